```python
import jax, jax.numpy as jnp
from jax import lax
import numpy as np

D_MODEL = 2048
BATCH = 4
SEQ = 2048
DEPTH = 4
DEC_BATCH = 128
DEC_SEQ = 8
PAST_LEN = 8192
PAGE_SIZE = 128

MLA_HEADS = 8
QK_NOPE_DIM = 128
QK_ROPE_DIM = 64
V_HEAD_DIM = 128
Q_LORA_RANK = 512
KV_LORA_RANK = 256
MLA_WIDTH = MLA_HEADS * V_HEAD_DIM
MLA_SCALE = (QK_NOPE_DIM + QK_ROPE_DIM) ** -0.5
ROPE_THETA = 10000.0
Q_BLOCK = 128
CHUNK = 128
CHUNK_HEADS = 4
CHUNK_HEAD_DIM = 128
CHUNK_WIDTH = CHUNK_HEADS * CHUNK_HEAD_DIM
N_MEM = 256
MEM_HEADS = 4
MEM_HEAD_DIM = 128
MEM_WIDTH = MEM_HEADS * MEM_HEAD_DIM
MEM_SCALE = MEM_HEAD_DIM ** -0.5
MIX_WIDTH = MLA_WIDTH + CHUNK_WIDTH + MEM_WIDTH
IN_WIDTH = Q_LORA_RANK + KV_LORA_RANK + QK_ROPE_DIM + 2 * CHUNK_WIDTH + MEM_WIDTH
D_FF = 5632
EPS = 1e-6

kernel_name = 'hymba_mla_gmlp_macaron_decode_step'


def rms_norm(x, g):
    xf = x.astype(jnp.float32)
    y = xf * lax.rsqrt(jnp.mean(xf * xf, axis=-1, keepdims=True) + EPS)
    return (y * g.astype(jnp.float32)).astype(x.dtype)


def swiglu_half(x, g, w_gate, w_up, w_down):
    h = rms_norm(x, g)
    return x + 0.5 * ((jax.nn.silu(h @ w_gate) * (h @ w_up)) @ w_down)


def rope_tables(pos):
    inv = ROPE_THETA ** (-jnp.arange(0, QK_ROPE_DIM, 2, dtype=jnp.float32) / QK_ROPE_DIM)
    ang = pos.astype(jnp.float32)[:, None] * inv[None, :]
    return jnp.cos(ang), jnp.sin(ang)


def apply_rope(x, cos, sin):
    x1, x2 = jnp.split(x.astype(jnp.float32), 2, axis=-1)
    c = cos[:, None, :]
    s = sin[:, None, :]
    return jnp.concatenate([x1 * c - x2 * s, x1 * s + x2 * c], axis=-1).astype(x.dtype)


def mixer_projections(h, pos, p):
    B, T, _ = h.shape
    z = h @ p['w_in']
    bounds = np.cumsum([Q_LORA_RANK, KV_LORA_RANK, QK_ROPE_DIM, CHUNK_WIDTH, CHUNK_WIDTH]).tolist()
    c_q, c_kv, k_r, u, v, q_m = jnp.split(z, bounds, axis=-1)
    cos, sin = rope_tables(pos)
    q = (rms_norm(c_q, p['q_lat_norm']) @ p['w_uq']).reshape(B, T, MLA_HEADS, QK_NOPE_DIM + QK_ROPE_DIM)
    q_nope = rms_norm(q[..., :QK_NOPE_DIM], p['q_nope_norm'])
    q_rope = apply_rope(rms_norm(q[..., QK_NOPE_DIM:], p['q_rope_norm']), cos, sin)
    lat = rms_norm(c_kv, p['kv_lat_norm'])
    k_rope = apply_rope(rms_norm(k_r, p['k_rope_norm'])[:, :, None, :], cos, sin)[:, :, 0, :]
    u = jax.nn.gelu(u)
    v = rms_norm(jax.nn.gelu(v), p['sgu_norm'])
    q_m = rms_norm(q_m.reshape(B, T, MEM_HEADS, MEM_HEAD_DIM), p['mem_q_norm'])
    return q_nope, q_rope, lat, k_rope, u, v, q_m


def latent_keys(lat, w_uk, g):
    return rms_norm(jnp.einsum('...r,rhd->...hd', lat, w_uk), g)


def mla_prompt(q_nope, q_rope, lat, k_rope, p):
    B, S = lat.shape[:2]
    k_nope = latent_keys(lat, p['w_uk'], p['k_nope_norm'])
    v = jnp.einsum('bsr,rhd->bshd', lat, p['w_uv'])
    n_blk = S // Q_BLOCK
    k_pos = jnp.arange(S)

    def to_blocks(a):
        return jnp.moveaxis(a.reshape(B, n_blk, Q_BLOCK, *a.shape[2:]), 1, 0)

    def block(args):
        qn, qr, i = args
        s = jnp.einsum('bqhd,bkhd->bhqk', qn, k_nope) + jnp.einsum('bqhd,bkd->bhqk', qr, k_rope)
        q_pos = i * Q_BLOCK + jnp.arange(Q_BLOCK)
        s = jnp.where(k_pos[None, :] <= q_pos[:, None], s.astype(jnp.float32) * MLA_SCALE, -jnp.inf)
        pr = jax.nn.softmax(s, axis=-1).astype(v.dtype)
        return jnp.einsum('bhqk,bkhd->bqhd', pr, v)

    o = lax.map(block, (to_blocks(q_nope), to_blocks(q_rope), jnp.arange(n_blk)))
    return jnp.moveaxis(o, 0, 1).reshape(B, S, MLA_WIDTH)


def mla_sample(q_nope, q_rope, lat, k_rope, cache_lat, cache_kr, page_table, layer, p):
    Bd, T = lat.shape[:2]

    def scores(lat_blk, kr_blk):
        kn = latent_keys(lat_blk, p['w_uk'], p['k_nope_norm'])
        s = jnp.einsum('bqhd,bkhd->bhqk', q_nope, kn) + jnp.einsum('bqhd,bkd->bhqk', q_rope, kr_blk)
        return s.astype(jnp.float32) * MLA_SCALE

    def merge(carry, s, lat_blk):
        m, l, acc = carry
        m_new = jnp.maximum(m, s.max(axis=-1))
        corr = jnp.exp(m - m_new)
        pr = jnp.exp(s - m_new[..., None])
        l_new = l * corr + pr.sum(axis=-1)
        acc_new = acc * corr[..., None] + jnp.einsum('bhqk,bkr->bhqr', pr, lat_blk.astype(jnp.float32))
        return (m_new, l_new, acc_new)

    def page_step(carry, pages):
        lat_blk = cache_lat[layer, pages]
        kr_blk = cache_kr[layer, pages]
        return merge(carry, scores(lat_blk, kr_blk), lat_blk), None

    init = (jnp.full((Bd, MLA_HEADS, T), -1e30, jnp.float32),
            jnp.zeros((Bd, MLA_HEADS, T), jnp.float32),
            jnp.zeros((Bd, MLA_HEADS, T, KV_LORA_RANK), jnp.float32))
    carry, _ = lax.scan(page_step, init, page_table.T)
    causal = jnp.tril(jnp.ones((T, T), dtype=bool))
    s_new = jnp.where(causal, scores(lat, k_rope), -jnp.inf)
    _, l, acc = merge(carry, s_new, lat)
    o_lat = (acc / l[..., None]).astype(lat.dtype)
    o = jnp.einsum('bhqr,rhd->bqhd', o_lat, p['w_uv'])
    return o.reshape(Bd, T, MLA_WIDTH)


def chunk_mlp_prompt(u, v, p):
    B, S, _ = v.shape
    vb = v.reshape(B, S // CHUNK, CHUNK, CHUNK_HEADS, CHUNK_HEAD_DIM)
    w = jnp.tril(p['w_spatial'])
    s = jnp.einsum('hts,bcshd->bcthd', w, vb) + p['b_spatial'].T[None, None, :, :, None]
    return u * s.reshape(B, S, CHUNK_WIDTH)


def chunk_mlp_sample(u, v, p):
    Bd, T, _ = v.shape
    vb = v.reshape(Bd, T, CHUNK_HEADS, CHUNK_HEAD_DIM)
    w = jnp.tril(p['w_spatial'][:, :T, :T])
    s = jnp.einsum('hts,bshd->bthd', w, vb) + p['b_spatial'][:, :T].T[None, :, :, None]
    return u * s.reshape(Bd, T, CHUNK_WIDTH)


def memory_kv(mem, p):
    B, M, _ = mem.shape
    kv = (rms_norm(mem, p['mem_norm']) @ p['w_mem_kv']).reshape(B, M, 2, MEM_HEADS, MEM_HEAD_DIM)
    return rms_norm(kv[:, :, 0], p['mem_k_norm']), kv[:, :, 1]


def mem_attention(q_m, k_m, v_m):
    B, T = q_m.shape[:2]
    s = jnp.einsum('bthd,bmhd->bhtm', q_m, k_m).astype(jnp.float32) * MEM_SCALE
    pr = jax.nn.softmax(s, axis=-1).astype(v_m.dtype)
    return jnp.einsum('bhtm,bmhd->bthd', pr, v_m).reshape(B, T, MEM_WIDTH)


def merge_heads(x, o_mla, o_chunk, o_mem, p):
    g_mla, g_chunk, g_mem = jnp.split(p['out_norm'], [MLA_WIDTH, MLA_WIDTH + CHUNK_WIDTH])
    o = jnp.concatenate([rms_norm(o_mla, g_mla), rms_norm(o_chunk, g_chunk), rms_norm(o_mem, g_mem)], axis=-1)
    return x + o @ p['w_o']


def setup_inputs(seed: int = 0) -> dict:
    key = jax.random.key(seed)
    ks = iter(jax.random.split(key, 48))
    f32 = jnp.float32
    nrm = lambda shape, scale: scale * jax.random.normal(next(ks), shape, f32)
    gain = lambda shape: 1.0 + 0.02 * jax.random.normal(next(ks), shape, f32)
    n_pages = PAST_LEN // PAGE_SIZE
    n_pool = (DEC_BATCH * n_pages * 5) // 4
    page_table = jax.random.permutation(next(ks), n_pool)[:DEC_BATCH * n_pages].reshape(DEC_BATCH, n_pages).astype(jnp.int32)
    L = DEPTH
    return {
        'x_prompt': nrm((BATCH, SEQ, D_MODEL), 1.0),
        'x_sample': nrm((DEC_BATCH, DEC_SEQ, D_MODEL), 1.0),
        'mem_prompt': nrm((BATCH, N_MEM, D_MODEL), 1.0),
        'cache_kv_latent': nrm((L, n_pool, PAGE_SIZE, KV_LORA_RANK), 1.0),
        'cache_k_rope': nrm((L, n_pool, PAGE_SIZE, QK_ROPE_DIM), 1.0),
        'cache_mem_k': nrm((L, DEC_BATCH, N_MEM, MEM_HEADS, MEM_HEAD_DIM), 1.0),
        'cache_mem_v': nrm((L, DEC_BATCH, N_MEM, MEM_HEADS, MEM_HEAD_DIM), 1.0),
        'page_table': page_table,
        'ffn1_norm': gain((L, D_MODEL)),
        'ffn1_w_gate': nrm((L, D_MODEL, D_FF), D_MODEL ** -0.5),
        'ffn1_w_up': nrm((L, D_MODEL, D_FF), D_MODEL ** -0.5),
        'ffn1_w_down': nrm((L, D_FF, D_MODEL), D_FF ** -0.5),
        'mix_norm': gain((L, D_MODEL)),
        'w_in': nrm((L, D_MODEL, IN_WIDTH), D_MODEL ** -0.5),
        'q_lat_norm': gain((L, Q_LORA_RANK)),
        'w_uq': nrm((L, Q_LORA_RANK, MLA_HEADS * (QK_NOPE_DIM + QK_ROPE_DIM)), Q_LORA_RANK ** -0.5),
        'q_nope_norm': gain((L, QK_NOPE_DIM)),
        'q_rope_norm': gain((L, QK_ROPE_DIM)),
        'kv_lat_norm': gain((L, KV_LORA_RANK)),
        'w_uk': nrm((L, KV_LORA_RANK, MLA_HEADS, QK_NOPE_DIM), KV_LORA_RANK ** -0.5),
        'w_uv': nrm((L, KV_LORA_RANK, MLA_HEADS, V_HEAD_DIM), KV_LORA_RANK ** -0.5),
        'k_nope_norm': gain((L, QK_NOPE_DIM)),
        'k_rope_norm': gain((L, QK_ROPE_DIM)),
        'sgu_norm': gain((L, CHUNK_WIDTH)),
        'w_spatial': nrm((L, CHUNK_HEADS, CHUNK, CHUNK), CHUNK ** -0.5),
        'b_spatial': 1.0 + nrm((L, CHUNK_HEADS, CHUNK), 0.1),
        'mem_norm': gain((L, D_MODEL)),
        'w_mem_kv': nrm((L, D_MODEL, 2 * MEM_WIDTH), D_MODEL ** -0.5),
        'mem_q_norm': gain((L, MEM_HEAD_DIM)),
        'mem_k_norm': gain((L, MEM_HEAD_DIM)),
        'out_norm': gain((L, MIX_WIDTH)),
        'w_o': nrm((L, MIX_WIDTH, D_MODEL), MIX_WIDTH ** -0.5),
        'ffn2_norm': gain((L, D_MODEL)),
        'ffn2_w_gate': nrm((L, D_MODEL, D_FF), D_MODEL ** -0.5),
        'ffn2_w_up': nrm((L, D_MODEL, D_FF), D_MODEL ** -0.5),
        'ffn2_w_down': nrm((L, D_FF, D_MODEL), D_FF ** -0.5),
    }


def reference(x_prompt, x_sample, mem_prompt, cache_kv_latent, cache_k_rope, cache_mem_k, cache_mem_v,
              page_table, ffn1_norm, ffn1_w_gate, ffn1_w_up, ffn1_w_down, mix_norm, w_in, q_lat_norm, w_uq,
              q_nope_norm, q_rope_norm, kv_lat_norm, w_uk, w_uv, k_nope_norm, k_rope_norm, sgu_norm,
              w_spatial, b_spatial, mem_norm, w_mem_kv, mem_q_norm, mem_k_norm, out_norm, w_o,
              ffn2_norm, ffn2_w_gate, ffn2_w_up, ffn2_w_down):
    pos_p = jnp.arange(x_prompt.shape[1], dtype=jnp.int32)
    pos_s = PAST_LEN + jnp.arange(x_sample.shape[1], dtype=jnp.int32)
    xp, xs = x_prompt, x_sample
    lat_p_all, kr_p_all, mk_p_all, mv_p_all = [], [], [], []
    lat_s_all, kr_s_all, v_s_all = [], [], []
    for l in range(DEPTH):
        p = {'w_in': w_in[l], 'q_lat_norm': q_lat_norm[l], 'w_uq': w_uq[l], 'q_nope_norm': q_nope_norm[l],
             'q_rope_norm': q_rope_norm[l], 'kv_lat_norm': kv_lat_norm[l], 'w_uk': w_uk[l], 'w_uv': w_uv[l],
             'k_nope_norm': k_nope_norm[l], 'k_rope_norm': k_rope_norm[l], 'sgu_norm': sgu_norm[l],
             'w_spatial': w_spatial[l], 'b_spatial': b_spatial[l], 'mem_norm': mem_norm[l],
             'w_mem_kv': w_mem_kv[l], 'mem_q_norm': mem_q_norm[l], 'mem_k_norm': mem_k_norm[l],
             'out_norm': out_norm[l], 'w_o': w_o[l]}
        xp = swiglu_half(xp, ffn1_norm[l], ffn1_w_gate[l], ffn1_w_up[l], ffn1_w_down[l])
        xs = swiglu_half(xs, ffn1_norm[l], ffn1_w_gate[l], ffn1_w_up[l], ffn1_w_down[l])
        qn, qr, lat, kr, u, v, qm = mixer_projections(rms_norm(xp, mix_norm[l]), pos_p, p)
        mk, mv = memory_kv(mem_prompt, p)
        xp = merge_heads(xp, mla_prompt(qn, qr, lat, kr, p), chunk_mlp_prompt(u, v, p),
                         mem_attention(qm, mk, mv), p)
        lat_p_all.append(lat)
        kr_p_all.append(kr)
        mk_p_all.append(mk)
        mv_p_all.append(mv)
        qn, qr, lat, kr, u, v, qm = mixer_projections(rms_norm(xs, mix_norm[l]), pos_s, p)
        xs = merge_heads(xs, mla_sample(qn, qr, lat, kr, cache_kv_latent, cache_k_rope, page_table, l, p),
                         chunk_mlp_sample(u, v, p), mem_attention(qm, cache_mem_k[l], cache_mem_v[l]), p)
        lat_s_all.append(lat)
        kr_s_all.append(kr)
        v_s_all.append(v)
        xp = swiglu_half(xp, ffn2_norm[l], ffn2_w_gate[l], ffn2_w_up[l], ffn2_w_down[l])
        xs = swiglu_half(xs, ffn2_norm[l], ffn2_w_gate[l], ffn2_w_up[l], ffn2_w_down[l])
    return (xp, xs, jnp.stack(lat_p_all), jnp.stack(kr_p_all), jnp.stack(mk_p_all), jnp.stack(mv_p_all),
            jnp.stack(lat_s_all), jnp.stack(kr_s_all), jnp.stack(v_s_all))
```

```python
import functools

import numpy as np
import jax
import jax.numpy as jnp
from jax import lax
from jax.experimental import pallas as pl
from jax.experimental.pallas import tpu as pltpu

F32 = jnp.float32
BF16 = jnp.bfloat16

D_MODEL = 2048
D_FF = 5632
MLA_HEADS = 8
QK_NOPE_DIM = 128
QK_ROPE_DIM = 64
V_HEAD_DIM = 128
Q_LORA_RANK = 512
KV_LORA_RANK = 256
MLA_WIDTH = MLA_HEADS * V_HEAD_DIM
MLA_SCALE = (QK_NOPE_DIM + QK_ROPE_DIM) ** -0.5
ROPE_THETA = 10000.0
PAGE_SIZE = 128
CHUNK = 128
CHUNK_HEADS = 4
CHUNK_WIDTH = 512
N_MEM = 256
MEM_HEADS = 4
MEM_HEAD_DIM = 128
MEM_WIDTH = 512
MEM_SCALE = MEM_HEAD_DIM ** -0.5
EPS = 1e-6

LANES = 128
QK_PAD = 2 * LANES
IN_PAD_WIDTH = Q_LORA_RANK + KV_LORA_RANK + 2 * CHUNK_WIDTH + MEM_WIDTH + LANES
VMEM_LIMIT = 56 * 1024 * 1024

_O_CQ, _O_KV, _O_U, _O_V, _O_QM, _O_KR = 0, 512, 768, 1280, 1792, 2304


def _params(*sem):
    return pltpu.CompilerParams(dimension_semantics=sem, vmem_limit_bytes=VMEM_LIMIT)


def _pick(n, cands):
    for c in cands:
        if n % c == 0:
            return c
    raise ValueError(f"no tile for {n}")


def _rms(x, g):
    ms = jnp.mean(x * x, axis=-1, keepdims=True)
    return x * lax.rsqrt(ms + EPS) * g


def _rms_pad(x, g, n):
    ms = jnp.sum(x * x, axis=-1, keepdims=True) * (1.0 / n)
    return x * lax.rsqrt(ms + EPS) * g


def _gelu(x):
    c = np.sqrt(2.0 / np.pi).astype(np.float32)
    return x * (0.5 * (1.0 + jnp.tanh(c * (x + 0.044715 * (x * x * x)))))


def _dot(a, b):
    return jnp.dot(a, b, preferred_element_type=F32)


def _dot_nt(a, b):
    return lax.dot_general(a, b, (((1,), (1,)), ((), ())), preferred_element_type=F32)


def _const_spec(shape, layer=None):
    nd = len(shape)
    if layer is None:
        return pl.BlockSpec(shape, lambda *_: (0,) * nd)
    return pl.BlockSpec((None,) + shape, lambda *_: (layer,) + (0,) * nd)


def _ffn_kernel(x_ref, g_ref, wg_ref, wu_ref, wd_ref, o_ref, h_ref):
    f = pl.program_id(1)

    @pl.when(f == 0)
    def _():
        h_ref[...] = _rms(x_ref[...], g_ref[...]).astype(BF16)
        o_ref[...] = jnp.zeros_like(o_ref)

    h = h_ref[...]
    a = _dot(h, wg_ref[...])
    b = _dot(h, wu_ref[...])
    act = (a * (1.0 / (1.0 + jnp.exp(-a))) * b).astype(BF16)
    o_ref[...] += _dot(act, wd_ref[...])

    @pl.when(f == pl.num_programs(1) - 1)
    def _():
        o_ref[...] = x_ref[...] + 0.5 * o_ref[...]


def _ffn(x, g, wg, wu, wd, layer):
    m = x.shape[0]
    tm = _pick(m, (512, 256, 128))
    tf = 512
    return pl.pallas_call(
        _ffn_kernel,
        grid=(m // tm, D_FF // tf),
        in_specs=[
            pl.BlockSpec((tm, D_MODEL), lambda i, f: (i, 0)),
            pl.BlockSpec((None, 1, D_MODEL), lambda i, f: (layer, 0, 0)),
            pl.BlockSpec((None, D_MODEL, tf), lambda i, f: (layer, 0, f)),
            pl.BlockSpec((None, D_MODEL, tf), lambda i, f: (layer, 0, f)),
            pl.BlockSpec((None, tf, D_MODEL), lambda i, f: (layer, f, 0)),
        ],
        out_specs=pl.BlockSpec((tm, D_MODEL), lambda i, f: (i, 0)),
        out_shape=jax.ShapeDtypeStruct((m, D_MODEL), F32),
        scratch_shapes=[pltpu.VMEM((tm, D_MODEL), BF16)],
        compiler_params=_params("parallel", "arbitrary"),
        name="ffn",
    )(x, g, wg, wu, wd)


def _mixer_kernel(x_ref, c_ref, s1_ref, s2_ref, gmix_ref, win_ref, gqlat_ref, wuq_ref, gqn_ref, gqr_ref,
                  gkv_ref, gkr_ref, gsgu_ref, gmq_ref, *rest, with_kv):
    if with_kv:
        wuk_ref, gkn_ref, wuv_ref, q_ref, lat_ref, kr_ref, u_ref, v_ref, qm_ref, k_ref, vv_ref = rest
    else:
        q_ref, lat_ref, kr_ref, u_ref, v_ref, qm_ref = rest
    h = _rms(x_ref[...], gmix_ref[...]).astype(BF16)
    z = _dot(h, win_ref[...])
    cos, sin_lo, sin_hi = c_ref[...], s1_ref[...], s2_ref[...]

    def rope(t):
        return t * cos + pltpu.roll(t, LANES - 32, 1) * sin_lo + pltpu.roll(t, 32, 1) * sin_hi

    cq = _rms(z[:, _O_CQ:_O_CQ + Q_LORA_RANK], gqlat_ref[...]).astype(BF16)
    q = _dot(cq, wuq_ref[...])
    for hd in range(MLA_HEADS):
        qn = _rms(q[:, hd * LANES:(hd + 1) * LANES], gqn_ref[...])
        qr = rope(_rms_pad(q[:, MLA_WIDTH + hd * LANES:MLA_WIDTH + (hd + 1) * LANES], gqr_ref[...], QK_ROPE_DIM))
        q_ref[:, hd * QK_PAD:hd * QK_PAD + LANES] = qn.astype(BF16)
        q_ref[:, hd * QK_PAD + LANES:(hd + 1) * QK_PAD] = qr.astype(BF16)

    lat = _rms(z[:, _O_KV:_O_KV + KV_LORA_RANK], gkv_ref[...])
    lat_ref[...] = lat
    kr = rope(_rms_pad(z[:, _O_KR:_O_KR + LANES], gkr_ref[...], QK_ROPE_DIM))
    kr_ref[...] = kr
    u_ref[...] = _gelu(z[:, _O_U:_O_U + CHUNK_WIDTH])
    v_ref[...] = _rms(_gelu(z[:, _O_V:_O_V + CHUNK_WIDTH]), gsgu_ref[...])
    for hd in range(MEM_HEADS):
        sl = slice(hd * LANES, (hd + 1) * LANES)
        qm_ref[:, sl] = _rms(z[:, _O_QM + hd * LANES:_O_QM + (hd + 1) * LANES], gmq_ref[...])

    if with_kv:
        latb = lat.astype(BF16)
        kn = _dot(latb, wuk_ref[...])
        krb = kr.astype(BF16)
        for hd in range(MLA_HEADS):
            k_ref[:, hd * QK_PAD:hd * QK_PAD + LANES] = _rms(kn[:, hd * LANES:(hd + 1) * LANES], gkn_ref[...]).astype(BF16)
            k_ref[:, hd * QK_PAD + LANES:(hd + 1) * QK_PAD] = krb
        vv_ref[...] = _dot(latb, wuv_ref[...]).astype(BF16)


def _mixer(x, tabs, w, layer, with_kv):
    m = x.shape[0]
    tm = _pick(m, (256, 128))
    row = lambda n: pl.BlockSpec((tm, n), lambda i: (i, 0))
    gain = lambda n: pl.BlockSpec((None, 1, n), lambda i: (layer, 0, 0))
    wspec = lambda r, c: pl.BlockSpec((None, r, c), lambda i: (layer, 0, 0))
    in_specs = [row(D_MODEL), row(LANES), row(LANES), row(LANES), gain(D_MODEL), wspec(D_MODEL, IN_PAD_WIDTH),
                gain(Q_LORA_RANK), wspec(Q_LORA_RANK, 2 * MLA_WIDTH), gain(LANES), gain(LANES),
                gain(KV_LORA_RANK), gain(LANES), gain(CHUNK_WIDTH), gain(LANES)]
    args = [x, *tabs, w["mix_norm"], w["w_in"], w["q_lat_norm"], w["w_uq"], w["q_nope_norm"], w["q_rope_norm"],
            w["kv_lat_norm"], w["k_rope_norm"], w["sgu_norm"], w["mem_q_norm"]]
    out_specs = [row(MLA_HEADS * QK_PAD), row(KV_LORA_RANK), row(LANES), row(CHUNK_WIDTH), row(CHUNK_WIDTH), row(MEM_WIDTH)]
    out_shape = [jax.ShapeDtypeStruct((m, MLA_HEADS * QK_PAD), BF16), jax.ShapeDtypeStruct((m, KV_LORA_RANK), F32),
                 jax.ShapeDtypeStruct((m, LANES), F32), jax.ShapeDtypeStruct((m, CHUNK_WIDTH), F32),
                 jax.ShapeDtypeStruct((m, CHUNK_WIDTH), F32), jax.ShapeDtypeStruct((m, MEM_WIDTH), F32)]
    if with_kv:
        in_specs += [wspec(KV_LORA_RANK, MLA_WIDTH), gain(LANES), wspec(KV_LORA_RANK, MLA_WIDTH)]
        args += [w["w_uk"], w["k_nope_norm"], w["w_uv"]]
        out_specs += [row(MLA_HEADS * QK_PAD), row(MLA_WIDTH)]
        out_shape += [jax.ShapeDtypeStruct((m, MLA_HEADS * QK_PAD), BF16), jax.ShapeDtypeStruct((m, MLA_WIDTH), BF16)]
    return pl.pallas_call(
        functools.partial(_mixer_kernel, with_kv=with_kv),
        grid=(m // tm,),
        in_specs=in_specs, out_specs=out_specs, out_shape=out_shape,
        compiler_params=_params("parallel"),
        name="mixer_kv" if with_kv else "mixer",
    )(*args)


def _mla_prompt_kernel(q_ref, k_ref, v_ref, o_ref, *, tq):
    qi = pl.program_id(2)
    q = q_ref[...]

    def step(kb, carry, masked):
        m, l, acc = carry
        off = pl.multiple_of(kb * tq, tq)
        s = _dot_nt(q, k_ref[pl.ds(off, tq), :]) * MLA_SCALE
        if masked:
            rows = lax.broadcasted_iota(jnp.int32, (tq, tq), 0)
            cols = lax.broadcasted_iota(jnp.int32, (tq, tq), 1)
            s = jnp.where(cols <= rows, s, -jnp.inf)
        m_new = jnp.maximum(m, jnp.max(s, axis=-1, keepdims=True))
        p = jnp.exp(s - m_new)
        corr = jnp.exp(m - m_new)
        l = l * corr + jnp.sum(p, axis=-1, keepdims=True)
        acc = acc * corr + _dot(p.astype(BF16), v_ref[pl.ds(off, tq), :])
        return m_new, l, acc

    init = (jnp.full((tq, 1), -1e30, F32), jnp.zeros((tq, 1), F32), jnp.zeros((tq, V_HEAD_DIM), F32))
    carry = lax.fori_loop(0, qi, lambda kb, c: step(kb, c, False), init)
    _, l, acc = step(qi, carry, True)
    o_ref[...] = acc / l


def _mla_prompt(q, k, v, batch, seq):
    tq = _pick(seq, (256, 128))
    nq = seq // tq
    return pl.pallas_call(
        functools.partial(_mla_prompt_kernel, tq=tq),
        grid=(batch, MLA_HEADS, nq),
        in_specs=[
            pl.BlockSpec((tq, QK_PAD), lambda b, h, i: (b * nq + i, h)),
            pl.BlockSpec((seq, QK_PAD), lambda b, h, i: (b, h)),
            pl.BlockSpec((seq, V_HEAD_DIM), lambda b, h, i: (b, h)),
        ],
        out_specs=pl.BlockSpec((tq, V_HEAD_DIM), lambda b, h, i: (b * nq + i, h)),
        out_shape=jax.ShapeDtypeStruct((batch * seq, MLA_WIDTH), F32),
        compiler_params=_params("parallel", "parallel", "arbitrary"),
        name="mla_prompt",
    )(q, k, v)


def _absorb_kernel(qn_ref, qr_ref, wuk_ref, g_ref, qa_ref, qro_ref, *, bd, t):
    wg = (wuk_ref[...] * g_ref[...]).astype(BF16)
    qa = _dot_nt(qn_ref[...], wg)
    qa_ref[...] = qa.reshape(bd, t, KV_LORA_RANK)
    qro_ref[...] = qr_ref[...].astype(F32).reshape(bd, t, LANES)


def _absorb(q, w, layer, bd, t):
    ms = bd * t
    return pl.pallas_call(
        functools.partial(_absorb_kernel, bd=bd, t=t),
        grid=(MLA_HEADS,),
        in_specs=[
            pl.BlockSpec((ms, LANES), lambda h: (0, 2 * h)),
            pl.BlockSpec((ms, LANES), lambda h: (0, 2 * h + 1)),
            pl.BlockSpec((None, KV_LORA_RANK, LANES), lambda h: (layer, 0, h)),
            pl.BlockSpec((None, 1, LANES), lambda h: (layer, 0, 0)),
        ],
        out_specs=[pl.BlockSpec((bd, t, KV_LORA_RANK), lambda h: (0, h, 0)),
                   pl.BlockSpec((bd, t, LANES), lambda h: (0, h, 0))],
        out_shape=[jax.ShapeDtypeStruct((bd, MLA_HEADS * t, KV_LORA_RANK), F32),
                   jax.ShapeDtypeStruct((bd, MLA_HEADS * t, LANES), F32)],
        compiler_params=_params("parallel"),
        name="mla_absorb",
    )(q, q, w["w_uk_f32"], w["k_nope_norm"])


def _mla_sample_kernel(pt_ref, qa_ref, qr_ref, wukt_ref, latn_ref, krn_ref, *rest, n_pg, t, sub):
    lat_refs = rest[:n_pg]
    kr_refs = rest[n_pg:2 * n_pg]
    o_ref, a_sc, latb_sc, krb_sc, s_sc, m_sc, l_sc, acc_sc = rest[2 * n_pg:]
    g = pl.program_id(1)
    nq = MLA_HEADS * t
    nk = MLA_HEADS * QK_NOPE_DIM
    tk = n_pg * PAGE_SIZE

    @pl.when(g == 0)
    def _():
        a_sc[0:nk, :] = wukt_ref[...]
        a_sc[nk:nk + nq, :] = qa_ref[0].astype(BF16)
        m_sc[...] = jnp.full_like(m_sc, -1e30)
        l_sc[...] = jnp.zeros_like(l_sc)
        acc_sc[...] = jnp.zeros_like(acc_sc)

    qr = qr_ref[0][:, 0:QK_ROPE_DIM].astype(BF16)

    def scores(latb, krb):
        st = _dot_nt(a_sc[...], latb)
        out = []
        for hd in range(MLA_HEADS):
            kn = st[hd * QK_NOPE_DIM:(hd + 1) * QK_NOPE_DIM, :]
            rn = lax.rsqrt(jnp.mean(kn * kn, axis=0, keepdims=True) + EPS)
            out.append(st[nk + hd * t:nk + (hd + 1) * t, :] * rn)
        return (jnp.concatenate(out, axis=0) + _dot_nt(qr, krb)) * MLA_SCALE

    def merge(s, latb):
        m_new = jnp.maximum(m_sc[...], jnp.max(s, axis=-1, keepdims=True))
        p = jnp.exp(s - m_new)
        corr = jnp.exp(m_sc[...] - m_new)
        l_sc[...] = l_sc[...] * corr + jnp.sum(p, axis=-1, keepdims=True)
        acc_sc[...] = acc_sc[...] * corr + _dot(p.astype(BF16), latb)
        m_sc[...] = m_new

    for j in range(n_pg):
        latb_sc[j * PAGE_SIZE:(j + 1) * PAGE_SIZE, :] = lat_refs[j][...].astype(BF16)
        krb_sc[j * PAGE_SIZE:(j + 1) * PAGE_SIZE, :] = kr_refs[j][...].astype(BF16)
    for c in range(tk // sub):
        sl = slice(c * sub, (c + 1) * sub)
        s_sc[:, sl] = scores(latb_sc[sl, :], krb_sc[sl, :])
    merge(s_sc[...], latb_sc[...])

    @pl.when(g == pl.num_programs(1) - 1)
    def _():
        latb_sc[0:PAGE_SIZE, :] = jnp.zeros((PAGE_SIZE, KV_LORA_RANK), BF16)
        krb_sc[0:PAGE_SIZE, :] = jnp.zeros((PAGE_SIZE, QK_ROPE_DIM), BF16)
        latb_sc[0:t, :] = latn_ref[...].astype(BF16)
        krb_sc[0:t, :] = krn_ref[:, 0:QK_ROPE_DIM].astype(BF16)
        newl = latb_sc[0:PAGE_SIZE, :]
        s = scores(newl, krb_sc[0:PAGE_SIZE, :])
        rows = lax.broadcasted_iota(jnp.int32, (nq, PAGE_SIZE), 0)
        cols = lax.broadcasted_iota(jnp.int32, (nq, PAGE_SIZE), 1)
        s = jnp.where(cols <= rows % t, s, -jnp.inf)
        merge(s, newl)
        o_ref[0] = acc_sc[...] / l_sc[...]


def _mla_sample(pt_flat, qa, qr, lat_new, kr_new, cache_lat, cache_kr, w, layer, bd, t, n_pages):
    n_pg = _pick(n_pages, (16, 8, 4, 2, 1))
    groups = n_pages // n_pg
    nq = MLA_HEADS * t
    tk = n_pg * PAGE_SIZE
    sub = min(tk, 512)

    def page_spec(j, width):
        return pl.BlockSpec((None, None, PAGE_SIZE, width),
                            lambda b, g, pt: (layer, pt[b * n_pages + g * n_pg + j], 0, 0))

    in_specs = [
        pl.BlockSpec((1, nq, KV_LORA_RANK), lambda b, g, pt: (b, 0, 0)),
        pl.BlockSpec((1, nq, LANES), lambda b, g, pt: (b, 0, 0)),
        pl.BlockSpec((None, MLA_HEADS * QK_NOPE_DIM, KV_LORA_RANK), lambda b, g, pt: (layer, 0, 0)),
        pl.BlockSpec((t, KV_LORA_RANK), lambda b, g, pt: (b, 0)),
        pl.BlockSpec((t, LANES), lambda b, g, pt: (b, 0)),
    ]
    in_specs += [page_spec(j, KV_LORA_RANK) for j in range(n_pg)]
    in_specs += [page_spec(j, QK_ROPE_DIM) for j in range(n_pg)]
    grid_spec = pltpu.PrefetchScalarGridSpec(
        num_scalar_prefetch=1,
        grid=(bd, groups),
        in_specs=in_specs,
        out_specs=pl.BlockSpec((1, nq, KV_LORA_RANK), lambda b, g, pt: (b, 0, 0)),
        scratch_shapes=[
            pltpu.VMEM((MLA_HEADS * QK_NOPE_DIM + nq, KV_LORA_RANK), BF16),
            pltpu.VMEM((tk, KV_LORA_RANK), BF16),
            pltpu.VMEM((tk, QK_ROPE_DIM), BF16),
            pltpu.VMEM((nq, tk), F32),
            pltpu.VMEM((nq, 1), F32),
            pltpu.VMEM((nq, 1), F32),
            pltpu.VMEM((nq, KV_LORA_RANK), F32),
        ],
    )
    return pl.pallas_call(
        functools.partial(_mla_sample_kernel, n_pg=n_pg, t=t, sub=sub),
        grid_spec=grid_spec,
        out_shape=jax.ShapeDtypeStruct((bd, nq, KV_LORA_RANK), F32),
        compiler_params=_params("parallel", "arbitrary"),
        name="mla_sample",
    )(pt_flat, qa, qr, w["w_uk_t"], lat_new, kr_new, *([cache_lat] * n_pg), *([cache_kr] * n_pg))


def _uv_up_kernel(o_ref, wuv_ref, out_ref, *, bd, t):
    o = o_ref[...].reshape(bd * t, KV_LORA_RANK).astype(BF16)
    out_ref[...] = _dot(o, wuv_ref[...])


def _uv_up(o_lat, w, layer, bd, t):
    return pl.pallas_call(
        functools.partial(_uv_up_kernel, bd=bd, t=t),
        grid=(MLA_HEADS,),
        in_specs=[pl.BlockSpec((bd, t, KV_LORA_RANK), lambda h: (0, h, 0)),
                  pl.BlockSpec((None, KV_LORA_RANK, V_HEAD_DIM), lambda h: (layer, 0, h))],
        out_specs=pl.BlockSpec((bd * t, V_HEAD_DIM), lambda h: (0, h)),
        out_shape=jax.ShapeDtypeStruct((bd * t, MLA_WIDTH), F32),
        compiler_params=_params("parallel"),
        name="mla_uv_up",
    )(o_lat, w["w_uv"])


def _chunk_kernel(u_ref, v_ref, w_ref, b_ref, o_ref):
    rows = lax.broadcasted_iota(jnp.int32, (CHUNK, CHUNK), 0)
    cols = lax.broadcasted_iota(jnp.int32, (CHUNK, CHUNK), 1)
    causal = cols <= rows
    for hd in range(CHUNK_HEADS):
        sl = slice(hd * LANES, (hd + 1) * LANES)
        wm = jnp.where(causal, w_ref[hd], 0.0).astype(BF16)
        s = _dot(wm, v_ref[:, sl].astype(BF16)) + b_ref[:, hd:hd + 1]
        o_ref[:, sl] = u_ref[:, sl] * s


def _chunk_mlp(u, v, w_sp, b_sp, layer):
    m = u.shape[0]
    return pl.pallas_call(
        _chunk_kernel,
        grid=(m // CHUNK,),
        in_specs=[pl.BlockSpec((CHUNK, CHUNK_WIDTH), lambda c: (c, 0)),
                  pl.BlockSpec((CHUNK, CHUNK_WIDTH), lambda c: (c, 0)),
                  pl.BlockSpec((None, CHUNK_HEADS, CHUNK, CHUNK), lambda c: (layer, 0, 0, 0)),
                  pl.BlockSpec((None, CHUNK, CHUNK_HEADS), lambda c: (layer, 0, 0))],
        out_specs=pl.BlockSpec((CHUNK, CHUNK_WIDTH), lambda c: (c, 0)),
        out_shape=jax.ShapeDtypeStruct((m, CHUNK_WIDTH), F32),
        compiler_params=_params("parallel"),
        name="chunk_mlp",
    )(u, v, w_sp, b_sp)


def _mem_kv_kernel(x_ref, g_ref, w_ref, gk_ref, k_ref, v_ref):
    h = _rms(x_ref[...], g_ref[...]).astype(BF16)
    kv = _dot(h, w_ref[...])
    for hd in range(MEM_HEADS):
        sl = slice(hd * LANES, (hd + 1) * LANES)
        k_ref[:, sl] = _rms(kv[:, sl], gk_ref[...])
    v_ref[...] = kv[:, MEM_WIDTH:]


def _mem_kv(mem, w, layer):
    m = mem.shape[0]
    tm = _pick(m, (256, 128))
    return pl.pallas_call(
        _mem_kv_kernel,
        grid=(m // tm,),
        in_specs=[pl.BlockSpec((tm, D_MODEL), lambda i: (i, 0)),
                  pl.BlockSpec((None, 1, D_MODEL), lambda i: (layer, 0, 0)),
                  pl.BlockSpec((None, D_MODEL, 2 * MEM_WIDTH), lambda i: (layer, 0, 0)),
                  pl.BlockSpec((None, 1, LANES), lambda i: (layer, 0, 0))],
        out_specs=[pl.BlockSpec((tm, MEM_WIDTH), lambda i: (i, 0))] * 2,
        out_shape=[jax.ShapeDtypeStruct((m, MEM_WIDTH), F32)] * 2,
        compiler_params=_params("parallel"),
        name="mem_kv",
    )(mem, w["mem_norm"], w["w_mem_kv"], w["mem_k_norm"])


def _softmax_rows(s):
    e = jnp.exp(s - jnp.max(s, axis=-1, keepdims=True))
    return e / jnp.sum(e, axis=-1, keepdims=True)


def _mem_attn_prompt_kernel(q_ref, k_ref, v_ref, o_ref):
    for hd in range(MEM_HEADS):
        sl = slice(hd * LANES, (hd + 1) * LANES)
        s = _dot_nt(q_ref[:, sl].astype(BF16), k_ref[:, sl].astype(BF16)) * MEM_SCALE
        o_ref[:, sl] = _dot(_softmax_rows(s).astype(BF16), v_ref[:, sl].astype(BF16))


def _mem_attn_prompt(qm, mk, mv, batch, seq):
    tq = _pick(seq, (1024, 512, 256, 128))
    nq = seq // tq
    return pl.pallas_call(
        _mem_attn_prompt_kernel,
        grid=(batch, nq),
        in_specs=[pl.BlockSpec((tq, MEM_WIDTH), lambda b, i: (b * nq + i, 0)),
                  pl.BlockSpec((N_MEM, MEM_WIDTH), lambda b, i: (b, 0)),
                  pl.BlockSpec((N_MEM, MEM_WIDTH), lambda b, i: (b, 0))],
        out_specs=pl.BlockSpec((tq, MEM_WIDTH), lambda b, i: (b * nq + i, 0)),
        out_shape=jax.ShapeDtypeStruct((batch * seq, MEM_WIDTH), F32),
        compiler_params=_params("parallel", "parallel"),
        name="mem_attn_prompt",
    )(qm, mk, mv)


def _mem_attn_sample_kernel(q_ref, k_ref, v_ref, o_ref, *, bb, t):
    for hd in range(MEM_HEADS):
        sl = slice(hd * LANES, (hd + 1) * LANES)
        q3 = q_ref[:, sl].reshape(bb, t, MEM_HEAD_DIM).astype(BF16)
        k3 = k_ref[:, :, sl].astype(BF16)
        v3 = v_ref[:, :, sl].astype(BF16)
        s = jnp.einsum("bqd,bkd->bqk", q3, k3, preferred_element_type=F32) * MEM_SCALE
        o3 = jnp.einsum("bqk,bkd->bqd", _softmax_rows(s).astype(BF16), v3, preferred_element_type=F32)
        o_ref[:, sl] = o3.reshape(bb * t, MEM_HEAD_DIM)


def _mem_attn_sample(qm, cache_k, cache_v, layer, bd, t):
    bb = _pick(bd, (16, 8, 4, 2, 1))
    kv_spec = pl.BlockSpec((None, bb, N_MEM, MEM_WIDTH), lambda i: (layer, i, 0, 0))
    return pl.pallas_call(
        functools.partial(_mem_attn_sample_kernel, bb=bb, t=t),
        grid=(bd // bb,),
        in_specs=[pl.BlockSpec((bb * t, MEM_WIDTH), lambda i: (i, 0)), kv_spec, kv_spec],
        out_specs=pl.BlockSpec((bb * t, MEM_WIDTH), lambda i: (i, 0)),
        out_shape=jax.ShapeDtypeStruct((bd * t, MEM_WIDTH), F32),
        compiler_params=_params("parallel"),
        name="mem_attn_sample",
    )(qm, cache_k, cache_v)


def _merge_kernel(x_ref, a_ref, c_ref, m_ref, g_ref, w_ref, o_ref):
    g = g_ref[...]
    c0, c1 = MLA_WIDTH, MLA_WIDTH + CHUNK_WIDTH
    acc = _dot(_rms(a_ref[...], g[:, 0:c0]).astype(BF16), w_ref[0:c0, :])
    acc += _dot(_rms(c_ref[...], g[:, c0:c1]).astype(BF16), w_ref[c0:c1, :])
    acc += _dot(_rms(m_ref[...], g[:, c1:]).astype(BF16), w_ref[c1:, :])
    o_ref[...] = x_ref[...] + acc


def _merge(x, o_mla, o_chunk, o_mem, w, layer):
    m = x.shape[0]
    tm = _pick(m, (512, 256, 128))
    row = lambda n: pl.BlockSpec((tm, n), lambda i: (i, 0))
    return pl.pallas_call(
        _merge_kernel,
        grid=(m // tm,),
        in_specs=[row(D_MODEL), row(MLA_WIDTH), row(CHUNK_WIDTH), row(MEM_WIDTH),
                  pl.BlockSpec((None, 1, D_MODEL), lambda i: (layer, 0, 0)),
                  pl.BlockSpec((None, D_MODEL, D_MODEL), lambda i: (layer, 0, 0))],
        out_specs=row(D_MODEL),
        out_shape=jax.ShapeDtypeStruct((m, D_MODEL), F32),
        compiler_params=_params("parallel"),
        name="merge_heads",
    )(x, o_mla, o_chunk, o_mem, w["out_norm"], w["w_o"])


def _rope_tables(pos):
    inv = ROPE_THETA ** (-jnp.arange(0, QK_ROPE_DIM, 2, dtype=F32) / QK_ROPE_DIM)
    ang = pos.astype(F32)[:, None] * inv[None, :]
    c, s = jnp.cos(ang), jnp.sin(ang)
    z32, z64 = jnp.zeros_like(c), jnp.zeros((pos.shape[0], LANES - QK_ROPE_DIM), F32)
    return (jnp.concatenate([c, c, z64], axis=-1),
            jnp.concatenate([-s, z32, z64], axis=-1),
            jnp.concatenate([z32, s, z64], axis=-1))


def _pad_lanes(a, width):
    return jnp.pad(a, [(0, 0)] * (a.ndim - 1) + [(0, width - a.shape[-1])])


def _prepare_weights(w_in, w_uq, w_uk, w_uv, w_spatial, b_spatial, bd, t, gains):
    depth = w_in.shape[0]
    b = np.cumsum([Q_LORA_RANK, KV_LORA_RANK, QK_ROPE_DIM, CHUNK_WIDTH, CHUNK_WIDTH]).tolist()
    c_q, c_kv, k_r, u, v, q_m = (w_in[..., :b[0]], w_in[..., b[0]:b[1]], w_in[..., b[1]:b[2]],
                                 w_in[..., b[2]:b[3]], w_in[..., b[3]:b[4]], w_in[..., b[4]:])
    w = {}
    w["w_in"] = jnp.concatenate([c_q, c_kv, u, v, q_m, _pad_lanes(k_r, LANES)], axis=-1).astype(BF16)
    uq = w_uq.reshape(depth, Q_LORA_RANK, MLA_HEADS, QK_NOPE_DIM + QK_ROPE_DIM)
    w["w_uq"] = jnp.concatenate(
        [uq[..., :QK_NOPE_DIM].reshape(depth, Q_LORA_RANK, MLA_WIDTH),
         _pad_lanes(uq[..., QK_NOPE_DIM:], LANES).reshape(depth, Q_LORA_RANK, MLA_WIDTH)], axis=-1).astype(BF16)
    uk = w_uk.reshape(depth, KV_LORA_RANK, MLA_HEADS * QK_NOPE_DIM)
    w["w_uk_f32"] = uk
    w["w_uk"] = uk.astype(BF16)
    w["w_uk_t"] = jnp.swapaxes(uk, 1, 2).astype(BF16)
    w["w_uv"] = w_uv.reshape(depth, KV_LORA_RANK, MLA_WIDTH).astype(BF16)
    reps = CHUNK // t
    eye = jnp.eye(reps, dtype=F32)
    w_blk = jnp.einsum("ab,lhts->lhatbs", eye, w_spatial[:, :, :t, :t]).reshape(depth, CHUNK_HEADS, CHUNK, CHUNK)
    w["w_sp_prompt"] = w_spatial
    w["w_sp_sample"] = w_blk
    w["b_sp_prompt"] = jnp.swapaxes(b_spatial, 1, 2)
    w["b_sp_sample"] = jnp.tile(jnp.swapaxes(b_spatial[:, :, :t], 1, 2), (1, reps, 1))
    for name, gval in gains.items():
        gval = gval[:, None, :]
        if gval.shape[-1] < LANES:
            gval = _pad_lanes(gval, LANES)
        w[name] = gval
    return w


def kernel(x_prompt, x_sample, mem_prompt, cache_kv_latent, cache_k_rope, cache_mem_k, cache_mem_v, page_table, ffn1_norm, ffn1_w_gate, ffn1_w_up, ffn1_w_down, mix_norm, w_in, q_lat_norm, w_uq, q_nope_norm, q_rope_norm, kv_lat_norm, w_uk, w_uv, k_nope_norm, k_rope_norm, sgu_norm, w_spatial, b_spatial, mem_norm, w_mem_kv, mem_q_norm, mem_k_norm, out_norm, w_o, ffn2_norm, ffn2_w_gate, ffn2_w_up, ffn2_w_down):
    batch, seq, _ = x_prompt.shape
    bd, t, _ = x_sample.shape
    depth = w_in.shape[0]
    n_pages = page_table.shape[1]
    past_len = n_pages * PAGE_SIZE
    assert (bd * t) % CHUNK == 0 and CHUNK % t == 0 and seq % CHUNK == 0

    gains = dict(mix_norm=mix_norm, q_lat_norm=q_lat_norm, q_nope_norm=q_nope_norm, q_rope_norm=q_rope_norm,
                 kv_lat_norm=kv_lat_norm, k_nope_norm=k_nope_norm, k_rope_norm=k_rope_norm, sgu_norm=sgu_norm,
                 mem_norm=mem_norm, mem_q_norm=mem_q_norm, mem_k_norm=mem_k_norm, out_norm=out_norm,
                 ffn1_norm=ffn1_norm, ffn2_norm=ffn2_norm)
    w = _prepare_weights(w_in, w_uq, w_uk, w_uv, w_spatial, b_spatial, bd, t, gains)
    w["w_mem_kv"] = w_mem_kv.astype(BF16)
    w["w_o"] = w_o.astype(BF16)
    ffn_w = [(w["ffn1_norm"], ffn1_w_gate.astype(BF16), ffn1_w_up.astype(BF16), ffn1_w_down.astype(BF16)),
             (w["ffn2_norm"], ffn2_w_gate.astype(BF16), ffn2_w_up.astype(BF16), ffn2_w_down.astype(BF16))]

    tabs_p = _rope_tables(jnp.tile(jnp.arange(seq, dtype=jnp.int32), batch))
    tabs_s = _rope_tables(jnp.tile(past_len + jnp.arange(t, dtype=jnp.int32), bd))
    pt_flat = page_table.reshape(-1)
    cache_mk = cache_mem_k.reshape(depth, bd, N_MEM, MEM_WIDTH)
    cache_mv = cache_mem_v.reshape(depth, bd, N_MEM, MEM_WIDTH)

    xp = x_prompt.reshape(batch * seq, D_MODEL)
    xs = x_sample.reshape(bd * t, D_MODEL)
    mem = mem_prompt.reshape(batch * N_MEM, D_MODEL)
    outs = [[] for _ in range(7)]
    for l in range(depth):
        xp = _ffn(xp, *ffn_w[0], l)
        xs = _ffn(xs, *ffn_w[0], l)
        q, lat, kr, u, v, qm, k, vv = _mixer(xp, tabs_p, w, l, True)
        mk, mv = _mem_kv(mem, w, l)
        o_mla = _mla_prompt(q, k, vv, batch, seq)
        o_chunk = _chunk_mlp(u, v, w["w_sp_prompt"], w["b_sp_prompt"], l)
        o_mem = _mem_attn_prompt(qm, mk, mv, batch, seq)
        xp = _merge(xp, o_mla, o_chunk, o_mem, w, l)
        outs[0].append(lat.reshape(batch, seq, KV_LORA_RANK))
        outs[1].append(kr[:, :QK_ROPE_DIM].reshape(batch, seq, QK_ROPE_DIM))
        outs[2].append(mk.reshape(batch, N_MEM, MEM_HEADS, MEM_HEAD_DIM))
        outs[3].append(mv.reshape(batch, N_MEM, MEM_HEADS, MEM_HEAD_DIM))
        q, lat, kr, u, v, qm = _mixer(xs, tabs_s, w, l, False)
        qa, qr = _absorb(q, w, l, bd, t)
        o_lat = _mla_sample(pt_flat, qa, qr, lat, kr, cache_kv_latent, cache_k_rope, w, l, bd, t, n_pages)
        o_mla = _uv_up(o_lat, w, l, bd, t)
        o_chunk = _chunk_mlp(u, v, w["w_sp_sample"], w["b_sp_sample"], l)
        o_mem = _mem_attn_sample(qm, cache_mk, cache_mv, l, bd, t)
        xs = _merge(xs, o_mla, o_chunk, o_mem, w, l)
        outs[4].append(lat.reshape(bd, t, KV_LORA_RANK))
        outs[5].append(kr[:, :QK_ROPE_DIM].reshape(bd, t, QK_ROPE_DIM))
        outs[6].append(v.reshape(bd, t, CHUNK_WIDTH))
        xp = _ffn(xp, *ffn_w[1], l)
        xs = _ffn(xs, *ffn_w[1], l)
    return (xp.reshape(batch, seq, D_MODEL), xs.reshape(bd, t, D_MODEL), *[jnp.stack(o) for o in outs])
```

```python
import functools

import numpy as np
import jax
import jax.numpy as jnp
from jax import lax
from jax.experimental import pallas as pl
from jax.experimental.pallas import tpu as pltpu

F32 = jnp.float32
BF16 = jnp.bfloat16

D_MODEL = 2048
D_FF = 5632
MLA_HEADS = 8
QK_NOPE_DIM = 128
QK_ROPE_DIM = 64
V_HEAD_DIM = 128
Q_LORA_RANK = 512
KV_LORA_RANK = 256
MLA_WIDTH = MLA_HEADS * V_HEAD_DIM
MLA_SCALE = (QK_NOPE_DIM + QK_ROPE_DIM) ** -0.5
ROPE_THETA = 10000.0
PAGE_SIZE = 128
CHUNK = 128
CHUNK_HEADS = 4
CHUNK_WIDTH = 512
N_MEM = 256
MEM_HEADS = 4
MEM_HEAD_DIM = 128
MEM_WIDTH = 512
MEM_SCALE = MEM_HEAD_DIM ** -0.5
EPS = 1e-6

LANES = 128
QK_PAD = 2 * LANES
IN_PAD_WIDTH = Q_LORA_RANK + KV_LORA_RANK + 2 * CHUNK_WIDTH + MEM_WIDTH + LANES
VMEM_LIMIT = 56 * 1024 * 1024

_O_CQ, _O_KV, _O_U, _O_V, _O_QM, _O_KR = 0, 512, 768, 1280, 1792, 2304


def _params(*sem):
    return pltpu.CompilerParams(dimension_semantics=sem, vmem_limit_bytes=VMEM_LIMIT)


def _pick(n, cands):
    for c in cands:
        if n % c == 0:
            return c
    raise ValueError(f"no tile for {n}")


def _rms(x, g):
    ms = jnp.mean(x * x, axis=-1, keepdims=True)
    return x * lax.rsqrt(ms + EPS) * g


def _rms_pad(x, g, n):
    ms = jnp.sum(x * x, axis=-1, keepdims=True) * (1.0 / n)
    return x * lax.rsqrt(ms + EPS) * g


def _gelu(x):
    c = np.sqrt(2.0 / np.pi).astype(np.float32)
    return x * (0.5 * (1.0 + jnp.tanh(c * (x + 0.044715 * (x * x * x)))))


def _dot(a, b):
    return jnp.dot(a, b, preferred_element_type=F32)


def _dot_nt(a, b):
    return lax.dot_general(a, b, (((1,), (1,)), ((), ())), preferred_element_type=F32)


def _const_spec(shape, layer=None):
    nd = len(shape)
    if layer is None:
        return pl.BlockSpec(shape, lambda *_: (0,) * nd)
    return pl.BlockSpec((None,) + shape, lambda *_: (layer,) + (0,) * nd)


def _ffn_kernel(x_ref, g_ref, wg_ref, wu_ref, wd_ref, o_ref, h_ref):
    f = pl.program_id(1)

    @pl.when(f == 0)
    def _():
        h_ref[...] = _rms(x_ref[...], g_ref[...]).astype(BF16)
        o_ref[...] = jnp.zeros_like(o_ref)

    h = h_ref[...]
    a = _dot(h, wg_ref[...])
    b = _dot(h, wu_ref[...])
    act = (a * (1.0 / (1.0 + jnp.exp(-a))) * b).astype(BF16)
    o_ref[...] += _dot(act, wd_ref[...])

    @pl.when(f == pl.num_programs(1) - 1)
    def _():
        o_ref[...] = x_ref[...] + 0.5 * o_ref[...]


def _ffn(x, g, wg, wu, wd, layer):
    m = x.shape[0]
    tm = _pick(m, (512, 256, 128))
    tf = 512
    return pl.pallas_call(
        _ffn_kernel,
        grid=(m // tm, D_FF // tf),
        in_specs=[
            pl.BlockSpec((tm, D_MODEL), lambda i, f: (i, 0)),
            pl.BlockSpec((None, 1, D_MODEL), lambda i, f: (layer, 0, 0)),
            pl.BlockSpec((None, D_MODEL, tf), lambda i, f: (layer, 0, f)),
            pl.BlockSpec((None, D_MODEL, tf), lambda i, f: (layer, 0, f)),
            pl.BlockSpec((None, tf, D_MODEL), lambda i, f: (layer, f, 0)),
        ],
        out_specs=pl.BlockSpec((tm, D_MODEL), lambda i, f: (i, 0)),
        out_shape=jax.ShapeDtypeStruct((m, D_MODEL), F32),
        scratch_shapes=[pltpu.VMEM((tm, D_MODEL), BF16)],
        compiler_params=_params("parallel", "arbitrary"),
        name="ffn",
    )(x, g, wg, wu, wd)


def _mixer_kernel(x_ref, c_ref, s1_ref, s2_ref, gmix_ref, win_ref, gqlat_ref, wuq_ref, gqn_ref, gqr_ref,
                  gkv_ref, gkr_ref, gsgu_ref, gmq_ref, *rest, with_kv):
    if with_kv:
        wuk_ref, gkn_ref, wuv_ref, q_ref, lat_ref, kr_ref, u_ref, v_ref, qm_ref, k_ref, vv_ref = rest
    else:
        q_ref, lat_ref, kr_ref, u_ref, v_ref, qm_ref = rest
    h = _rms(x_ref[...], gmix_ref[...]).astype(BF16)
    z = _dot(h, win_ref[...])
    cos, sin_lo, sin_hi = c_ref[...], s1_ref[...], s2_ref[...]

    def rope(t):
        return t * cos + pltpu.roll(t, LANES - 32, 1) * sin_lo + pltpu.roll(t, 32, 1) * sin_hi

    cq = _rms(z[:, _O_CQ:_O_CQ + Q_LORA_RANK], gqlat_ref[...]).astype(BF16)
    q = _dot(cq, wuq_ref[...])
    for hd in range(MLA_HEADS):
        qn = _rms(q[:, hd * LANES:(hd + 1) * LANES], gqn_ref[...])
        qr = rope(_rms_pad(q[:, MLA_WIDTH + hd * LANES:MLA_WIDTH + (hd + 1) * LANES], gqr_ref[...], QK_ROPE_DIM))
        q_ref[:, hd * QK_PAD:hd * QK_PAD + LANES] = qn.astype(BF16)
        q_ref[:, hd * QK_PAD + LANES:(hd + 1) * QK_PAD] = qr.astype(BF16)

    lat = _rms(z[:, _O_KV:_O_KV + KV_LORA_RANK], gkv_ref[...])
    lat_ref[...] = lat
    kr = rope(_rms_pad(z[:, _O_KR:_O_KR + LANES], gkr_ref[...], QK_ROPE_DIM))
    kr_ref[...] = kr
    u_ref[...] = _gelu(z[:, _O_U:_O_U + CHUNK_WIDTH])
    v_ref[...] = _rms(_gelu(z[:, _O_V:_O_V + CHUNK_WIDTH]), gsgu_ref[...])
    for hd in range(MEM_HEADS):
        sl = slice(hd * LANES, (hd + 1) * LANES)
        qm_ref[:, sl] = _rms(z[:, _O_QM + hd * LANES:_O_QM + (hd + 1) * LANES], gmq_ref[...])

    if with_kv:
        latb = lat.astype(BF16)
        kn = _dot(latb, wuk_ref[...])
        krb = kr.astype(BF16)
        for hd in range(MLA_HEADS):
            k_ref[:, hd * QK_PAD:hd * QK_PAD + LANES] = _rms(kn[:, hd * LANES:(hd + 1) * LANES], gkn_ref[...]).astype(BF16)
            k_ref[:, hd * QK_PAD + LANES:(hd + 1) * QK_PAD] = krb
        vv_ref[...] = _dot(latb, wuv_ref[...]).astype(BF16)


def _mixer(x, tabs, w, layer, with_kv):
    m = x.shape[0]
    tm = _pick(m, (256, 128))
    row = lambda n: pl.BlockSpec((tm, n), lambda i: (i, 0))
    gain = lambda n: pl.BlockSpec((None, 1, n), lambda i: (layer, 0, 0))
    wspec = lambda r, c: pl.BlockSpec((None, r, c), lambda i: (layer, 0, 0))
    in_specs = [row(D_MODEL), row(LANES), row(LANES), row(LANES), gain(D_MODEL), wspec(D_MODEL, IN_PAD_WIDTH),
                gain(Q_LORA_RANK), wspec(Q_LORA_RANK, 2 * MLA_WIDTH), gain(LANES), gain(LANES),
                gain(KV_LORA_RANK), gain(LANES), gain(CHUNK_WIDTH), gain(LANES)]
    args = [x, *tabs, w["mix_norm"], w["w_in"], w["q_lat_norm"], w["w_uq"], w["q_nope_norm"], w["q_rope_norm"],
            w["kv_lat_norm"], w["k_rope_norm"], w["sgu_norm"], w["mem_q_norm"]]
    out_specs = [row(MLA_HEADS * QK_PAD), row(KV_LORA_RANK), row(LANES), row(CHUNK_WIDTH), row(CHUNK_WIDTH), row(MEM_WIDTH)]
    out_shape = [jax.ShapeDtypeStruct((m, MLA_HEADS * QK_PAD), BF16), jax.ShapeDtypeStruct((m, KV_LORA_RANK), F32),
                 jax.ShapeDtypeStruct((m, LANES), F32), jax.ShapeDtypeStruct((m, CHUNK_WIDTH), F32),
                 jax.ShapeDtypeStruct((m, CHUNK_WIDTH), F32), jax.ShapeDtypeStruct((m, MEM_WIDTH), F32)]
    if with_kv:
        in_specs += [wspec(KV_LORA_RANK, MLA_WIDTH), gain(LANES), wspec(KV_LORA_RANK, MLA_WIDTH)]
        args += [w["w_uk"], w["k_nope_norm"], w["w_uv"]]
        out_specs += [row(MLA_HEADS * QK_PAD), row(MLA_WIDTH)]
        out_shape += [jax.ShapeDtypeStruct((m, MLA_HEADS * QK_PAD), BF16), jax.ShapeDtypeStruct((m, MLA_WIDTH), BF16)]
    return pl.pallas_call(
        functools.partial(_mixer_kernel, with_kv=with_kv),
        grid=(m // tm,),
        in_specs=in_specs, out_specs=out_specs, out_shape=out_shape,
        compiler_params=_params("parallel"),
        name="mixer_kv" if with_kv else "mixer",
    )(*args)


def _mla_prompt_kernel(q_ref, k_ref, v_ref, o_ref, *, seq, tq):
    rows = lax.broadcasted_iota(jnp.int32, (tq, tq), 0)
    cols = lax.broadcasted_iota(jnp.int32, (tq, tq), 1)
    causal = cols <= rows
    for qi in range(seq // tq):
        q = q_ref[qi * tq:(qi + 1) * tq, :]
        m = jnp.full((tq, 1), -1e30, F32)
        l = jnp.zeros((tq, 1), F32)
        acc = jnp.zeros((tq, V_HEAD_DIM), F32)
        for kb in range(qi + 1):
            s = _dot_nt(q, k_ref[kb * tq:(kb + 1) * tq, :]) * MLA_SCALE
            if kb == qi:
                s = jnp.where(causal, s, -jnp.inf)
            m_new = jnp.maximum(m, jnp.max(s, axis=-1, keepdims=True))
            p = jnp.exp(s - m_new)
            corr = jnp.exp(m - m_new)
            l = l * corr + jnp.sum(p, axis=-1, keepdims=True)
            acc = acc * corr + _dot(p.astype(BF16), v_ref[kb * tq:(kb + 1) * tq, :])
            m = m_new
        o_ref[qi * tq:(qi + 1) * tq, :] = acc / l


def _mla_prompt(q, k, v, batch, seq):
    tq = _pick(seq, (512, 256, 128))
    return pl.pallas_call(
        functools.partial(_mla_prompt_kernel, seq=seq, tq=tq),
        grid=(batch, MLA_HEADS),
        in_specs=[
            pl.BlockSpec((seq, QK_PAD), lambda b, h: (b, h)),
            pl.BlockSpec((seq, QK_PAD), lambda b, h: (b, h)),
            pl.BlockSpec((seq, V_HEAD_DIM), lambda b, h: (b, h)),
        ],
        out_specs=pl.BlockSpec((seq, V_HEAD_DIM), lambda b, h: (b, h)),
        out_shape=jax.ShapeDtypeStruct((batch * seq, MLA_WIDTH), F32),
        compiler_params=_params("parallel", "parallel"),
        name="mla_prompt",
    )(q, k, v)


def _absorb_kernel(qn_ref, qr_ref, wuk_ref, g_ref, qa_ref, qro_ref, *, bd, t):
    wg = (wuk_ref[...] * g_ref[...]).astype(BF16)
    qa = _dot_nt(qn_ref[...], wg)
    qa_ref[...] = qa.reshape(bd, t, KV_LORA_RANK)
    qro_ref[...] = qr_ref[...].astype(F32).reshape(bd, t, LANES)


def _absorb(q, w, layer, bd, t):
    ms = bd * t
    return pl.pallas_call(
        functools.partial(_absorb_kernel, bd=bd, t=t),
        grid=(MLA_HEADS,),
        in_specs=[
            pl.BlockSpec((ms, LANES), lambda h: (0, 2 * h)),
            pl.BlockSpec((ms, LANES), lambda h: (0, 2 * h + 1)),
            pl.BlockSpec((None, KV_LORA_RANK, LANES), lambda h: (layer, 0, h)),
            pl.BlockSpec((None, 1, LANES), lambda h: (layer, 0, 0)),
        ],
        out_specs=[pl.BlockSpec((bd, t, KV_LORA_RANK), lambda h: (0, h, 0)),
                   pl.BlockSpec((bd, t, LANES), lambda h: (0, h, 0))],
        out_shape=[jax.ShapeDtypeStruct((bd, MLA_HEADS * t, KV_LORA_RANK), F32),
                   jax.ShapeDtypeStruct((bd, MLA_HEADS * t, LANES), F32)],
        compiler_params=_params("parallel"),
        name="mla_absorb",
    )(q, q, w["w_uk_f32"], w["k_nope_norm"])


def _mla_sample_kernel(pt_ref, qa_ref, qr_ref, wukt_ref, latn_ref, krn_ref, clat_hbm, ckr_hbm, o_ref,
                       lat_buf, kr_buf, sem, a_sc, latb_sc, new_lat, new_kr, s_sc, *, layer, n_pages, t, ts):
    b = pl.program_id(0)
    slot = b % 2
    nq = MLA_HEADS * t
    nk = MLA_HEADS * QK_NOPE_DIM
    n_sub = n_pages * PAGE_SIZE // ts
    pages_per_sub = ts // PAGE_SIZE

    def page_copies(bb, sl, j):
        pg = pt_ref[bb * n_pages + j]
        row0 = pl.multiple_of(j * PAGE_SIZE, PAGE_SIZE)
        return (pltpu.make_async_copy(clat_hbm.at[layer, pg], lat_buf.at[sl, pl.ds(row0, PAGE_SIZE)], sem.at[0, sl]),
                pltpu.make_async_copy(ckr_hbm.at[layer, pg], kr_buf.at[sl, j], sem.at[1, sl]))

    @pl.when(b == 0)
    def _():
        a_sc[0:nk, :] = wukt_ref[...]

        def body(j, c):
            for cp in page_copies(0, 0, j):
                cp.start()
            return c
        lax.fori_loop(0, n_pages, body, 0)

    nxt = (b + 1) % pl.num_programs(0)

    def wait_body(j, c):
        for cp in page_copies(b, slot, j):
            cp.wait()
        return c
    lax.fori_loop(0, n_pages, wait_body, 0)

    a_sc[nk:nk + nq, :] = qa_ref[0].astype(BF16)
    qr = qr_ref[0][:, 0:QK_ROPE_DIM].astype(BF16)

    def nope_scores(latb):
        st = _dot_nt(a_sc[...], latb)
        out = []
        for hd in range(MLA_HEADS):
            kn = st[hd * QK_NOPE_DIM:(hd + 1) * QK_NOPE_DIM, :]
            rn = lax.rsqrt(jnp.mean(kn * kn, axis=0, keepdims=True) + EPS)
            out.append(st[nk + hd * t:nk + (hd + 1) * t, :] * rn)
        return jnp.concatenate(out, axis=0)

    def sub_body(i, c):
        for jj in range(pages_per_sub):
            for cp in page_copies(nxt, 1 - slot, i * pages_per_sub + jj):
                cp.start()
        row0 = pl.multiple_of(i * ts, ts)
        latb = lat_buf[slot, pl.ds(row0, ts), :].astype(BF16)
        latb_sc[pl.ds(row0, ts), :] = latb
        krt = jnp.concatenate(
            [kr_buf[slot, i * pages_per_sub + jj].astype(BF16) for jj in range(pages_per_sub)], axis=1)
        s_sc[i] = (nope_scores(latb) + _dot(qr, krt)) * MLA_SCALE
        return c
    lax.fori_loop(0, n_sub, sub_body, 0, unroll=_pick(n_sub, (4, 2, 1)))

    @pl.when(b == pl.num_programs(0) - 1)
    def _():
        def body(j, c):
            for cp in page_copies(nxt, 1 - slot, j):
                cp.wait()
            return c
        lax.fori_loop(0, n_pages, body, 0)

    new_lat[...] = jnp.zeros_like(new_lat)
    new_kr[...] = jnp.zeros_like(new_kr)
    new_lat[0:t, :] = latn_ref[...]
    new_kr[0:t, :] = krn_ref[:, 0:QK_ROPE_DIM]
    newl = new_lat[...].astype(BF16)
    s_new = (nope_scores(newl) + _dot_nt(qr, new_kr[...].astype(BF16))) * MLA_SCALE
    rows = lax.broadcasted_iota(jnp.int32, (nq, PAGE_SIZE), 0)
    cols = lax.broadcasted_iota(jnp.int32, (nq, PAGE_SIZE), 1)
    s_new = jnp.where(cols <= rows % t, s_new, -jnp.inf)

    lane_tiles = [(i, c * LANES) for i in range(n_sub) for c in range(ts // LANES)]
    smax = s_new
    for i, c0 in lane_tiles:
        smax = jnp.maximum(smax, s_sc[i, :, c0:c0 + LANES])
    m = jnp.max(smax, axis=-1, keepdims=True)
    p_new = jnp.exp(s_new - m)
    acc = _dot(p_new.astype(BF16), newl)
    psum = p_new
    for i in range(n_sub):
        p = jnp.exp(s_sc[i] - m)
        for c in range(ts // LANES):
            psum = psum + p[:, c * LANES:(c + 1) * LANES]
        acc = acc + _dot(p.astype(BF16), latb_sc[i * ts:(i + 1) * ts, :])
    o_ref[0] = acc / jnp.sum(psum, axis=-1, keepdims=True)


def _mla_sample(pt_flat, qa, qr, lat_new, kr_new, cache_lat, cache_kr_t, w, layer, bd, t, n_pages):
    nq = MLA_HEADS * t
    tk = n_pages * PAGE_SIZE
    ts = _pick(tk, (512, 256, 128))
    grid_spec = pltpu.PrefetchScalarGridSpec(
        num_scalar_prefetch=1,
        grid=(bd,),
        in_specs=[
            pl.BlockSpec((1, nq, KV_LORA_RANK), lambda b, pt: (b, 0, 0)),
            pl.BlockSpec((1, nq, LANES), lambda b, pt: (b, 0, 0)),
            pl.BlockSpec((None, MLA_HEADS * QK_NOPE_DIM, KV_LORA_RANK), lambda b, pt: (layer, 0, 0)),
            pl.BlockSpec((t, KV_LORA_RANK), lambda b, pt: (b, 0)),
            pl.BlockSpec((t, LANES), lambda b, pt: (b, 0)),
            pl.BlockSpec(memory_space=pl.ANY),
            pl.BlockSpec(memory_space=pl.ANY),
        ],
        out_specs=pl.BlockSpec((1, nq, KV_LORA_RANK), lambda b, pt: (b, 0, 0)),
        scratch_shapes=[
            pltpu.VMEM((2, tk, KV_LORA_RANK), F32),
            pltpu.VMEM((2, n_pages, QK_ROPE_DIM, PAGE_SIZE), F32),
            pltpu.SemaphoreType.DMA((2, 2)),
            pltpu.VMEM((MLA_HEADS * QK_NOPE_DIM + nq, KV_LORA_RANK), BF16),
            pltpu.VMEM((tk, KV_LORA_RANK), BF16),
            pltpu.VMEM((PAGE_SIZE, KV_LORA_RANK), F32),
            pltpu.VMEM((PAGE_SIZE, QK_ROPE_DIM), F32),
            pltpu.VMEM((tk // ts, nq, ts), F32),
        ],
    )
    return pl.pallas_call(
        functools.partial(_mla_sample_kernel, layer=layer, n_pages=n_pages, t=t, ts=ts),
        grid_spec=grid_spec,
        out_shape=jax.ShapeDtypeStruct((bd, nq, KV_LORA_RANK), F32),
        compiler_params=_params("arbitrary"),
        name="mla_sample",
    )(pt_flat, qa, qr, w["w_uk_t"], lat_new, kr_new, cache_lat, cache_kr_t)


def _uv_up_kernel(o_ref, wuv_ref, out_ref, *, bd, t):
    o = o_ref[...].reshape(bd * t, KV_LORA_RANK).astype(BF16)
    out_ref[...] = _dot(o, wuv_ref[...])


def _uv_up(o_lat, w, layer, bd, t):
    return pl.pallas_call(
        functools.partial(_uv_up_kernel, bd=bd, t=t),
        grid=(MLA_HEADS,),
        in_specs=[pl.BlockSpec((bd, t, KV_LORA_RANK), lambda h: (0, h, 0)),
                  pl.BlockSpec((None, KV_LORA_RANK, V_HEAD_DIM), lambda h: (layer, 0, h))],
        out_specs=pl.BlockSpec((bd * t, V_HEAD_DIM), lambda h: (0, h)),
        out_shape=jax.ShapeDtypeStruct((bd * t, MLA_WIDTH), F32),
        compiler_params=_params("parallel"),
        name="mla_uv_up",
    )(o_lat, w["w_uv"])


def _chunk_kernel(u_ref, v_ref, w_ref, b_ref, o_ref):
    rows = lax.broadcasted_iota(jnp.int32, (CHUNK, CHUNK), 0)
    cols = lax.broadcasted_iota(jnp.int32, (CHUNK, CHUNK), 1)
    causal = cols <= rows
    for hd in range(CHUNK_HEADS):
        sl = slice(hd * LANES, (hd + 1) * LANES)
        wm = jnp.where(causal, w_ref[hd], 0.0).astype(BF16)
        s = _dot(wm, v_ref[:, sl].astype(BF16)) + b_ref[:, hd:hd + 1]
        o_ref[:, sl] = u_ref[:, sl] * s


def _chunk_mlp(u, v, w_sp, b_sp, layer):
    m = u.shape[0]
    return pl.pallas_call(
        _chunk_kernel,
        grid=(m // CHUNK,),
        in_specs=[pl.BlockSpec((CHUNK, CHUNK_WIDTH), lambda c: (c, 0)),
                  pl.BlockSpec((CHUNK, CHUNK_WIDTH), lambda c: (c, 0)),
                  pl.BlockSpec((None, CHUNK_HEADS, CHUNK, CHUNK), lambda c: (layer, 0, 0, 0)),
                  pl.BlockSpec((None, CHUNK, CHUNK_HEADS), lambda c: (layer, 0, 0))],
        out_specs=pl.BlockSpec((CHUNK, CHUNK_WIDTH), lambda c: (c, 0)),
        out_shape=jax.ShapeDtypeStruct((m, CHUNK_WIDTH), F32),
        compiler_params=_params("parallel"),
        name="chunk_mlp",
    )(u, v, w_sp, b_sp)


def _mem_kv_kernel(x_ref, g_ref, w_ref, gk_ref, k_ref, v_ref):
    h = _rms(x_ref[...], g_ref[...]).astype(BF16)
    kv = _dot(h, w_ref[...])
    for hd in range(MEM_HEADS):
        sl = slice(hd * LANES, (hd + 1) * LANES)
        k_ref[:, sl] = _rms(kv[:, sl], gk_ref[...])
    v_ref[...] = kv[:, MEM_WIDTH:]


def _mem_kv(mem, w, layer):
    m = mem.shape[0]
    tm = _pick(m, (256, 128))
    return pl.pallas_call(
        _mem_kv_kernel,
        grid=(m // tm,),
        in_specs=[pl.BlockSpec((tm, D_MODEL), lambda i: (i, 0)),
                  pl.BlockSpec((None, 1, D_MODEL), lambda i: (layer, 0, 0)),
                  pl.BlockSpec((None, D_MODEL, 2 * MEM_WIDTH), lambda i: (layer, 0, 0)),
                  pl.BlockSpec((None, 1, LANES), lambda i: (layer, 0, 0))],
        out_specs=[pl.BlockSpec((tm, MEM_WIDTH), lambda i: (i, 0))] * 2,
        out_shape=[jax.ShapeDtypeStruct((m, MEM_WIDTH), F32)] * 2,
        compiler_params=_params("parallel"),
        name="mem_kv",
    )(mem, w["mem_norm"], w["w_mem_kv"], w["mem_k_norm"])


def _softmax_rows(s):
    e = jnp.exp(s - jnp.max(s, axis=-1, keepdims=True))
    return e / jnp.sum(e, axis=-1, keepdims=True)


def _mem_attn_prompt_kernel(q_ref, k_ref, v_ref, o_ref):
    for hd in range(MEM_HEADS):
        sl = slice(hd * LANES, (hd + 1) * LANES)
        s = _dot_nt(q_ref[:, sl].astype(BF16), k_ref[:, sl].astype(BF16)) * MEM_SCALE
        o_ref[:, sl] = _dot(_softmax_rows(s).astype(BF16), v_ref[:, sl].astype(BF16))


def _mem_attn_prompt(qm, mk, mv, batch, seq):
    tq = _pick(seq, (1024, 512, 256, 128))
    nq = seq // tq
    return pl.pallas_call(
        _mem_attn_prompt_kernel,
        grid=(batch, nq),
        in_specs=[pl.BlockSpec((tq, MEM_WIDTH), lambda b, i: (b * nq + i, 0)),
                  pl.BlockSpec((N_MEM, MEM_WIDTH), lambda b, i: (b, 0)),
                  pl.BlockSpec((N_MEM, MEM_WIDTH), lambda b, i: (b, 0))],
        out_specs=pl.BlockSpec((tq, MEM_WIDTH), lambda b, i: (b * nq + i, 0)),
        out_shape=jax.ShapeDtypeStruct((batch * seq, MEM_WIDTH), F32),
        compiler_params=_params("parallel", "parallel"),
        name="mem_attn_prompt",
    )(qm, mk, mv)


def _mem_attn_sample_kernel(q_ref, k_ref, v_ref, o_ref, *, bb, t):
    for hd in range(MEM_HEADS):
        sl = slice(hd * LANES, (hd + 1) * LANES)
        q3 = q_ref[:, sl].reshape(bb, t, MEM_HEAD_DIM).astype(BF16)
        k3 = k_ref[:, pl.ds(hd, N_MEM, stride=MEM_HEADS), :].astype(BF16)
        v3 = v_ref[:, pl.ds(hd, N_MEM, stride=MEM_HEADS), :].astype(BF16)
        s = jnp.einsum("bqd,bkd->bqk", q3, k3, preferred_element_type=F32) * MEM_SCALE
        o3 = jnp.einsum("bqk,bkd->bqd", _softmax_rows(s).astype(BF16), v3, preferred_element_type=F32)
        o_ref[:, sl] = o3.reshape(bb * t, MEM_HEAD_DIM)


def _mem_attn_sample(qm, cache_k, cache_v, layer, bd, t):
    bb = _pick(bd, (16, 8, 4, 2, 1))
    kv_spec = pl.BlockSpec((None, bb, N_MEM * MEM_HEADS, MEM_HEAD_DIM), lambda i: (layer, i, 0, 0))
    return pl.pallas_call(
        functools.partial(_mem_attn_sample_kernel, bb=bb, t=t),
        grid=(bd // bb,),
        in_specs=[pl.BlockSpec((bb * t, MEM_WIDTH), lambda i: (i, 0)), kv_spec, kv_spec],
        out_specs=pl.BlockSpec((bb * t, MEM_WIDTH), lambda i: (i, 0)),
        out_shape=jax.ShapeDtypeStruct((bd * t, MEM_WIDTH), F32),
        compiler_params=_params("parallel"),
        name="mem_attn_sample",
    )(qm, cache_k, cache_v)


def _merge_kernel(x_ref, a_ref, c_ref, m_ref, g_ref, w_ref, o_ref):
    g = g_ref[...]
    c0, c1 = MLA_WIDTH, MLA_WIDTH + CHUNK_WIDTH
    acc = _dot(_rms(a_ref[...], g[:, 0:c0]).astype(BF16), w_ref[0:c0, :])
    acc += _dot(_rms(c_ref[...], g[:, c0:c1]).astype(BF16), w_ref[c0:c1, :])
    acc += _dot(_rms(m_ref[...], g[:, c1:]).astype(BF16), w_ref[c1:, :])
    o_ref[...] = x_ref[...] + acc


def _merge(x, o_mla, o_chunk, o_mem, w, layer):
    m = x.shape[0]
    tm = _pick(m, (512, 256, 128))
    row = lambda n: pl.BlockSpec((tm, n), lambda i: (i, 0))
    return pl.pallas_call(
        _merge_kernel,
        grid=(m // tm,),
        in_specs=[row(D_MODEL), row(MLA_WIDTH), row(CHUNK_WIDTH), row(MEM_WIDTH),
                  pl.BlockSpec((None, 1, D_MODEL), lambda i: (layer, 0, 0)),
                  pl.BlockSpec((None, D_MODEL, D_MODEL), lambda i: (layer, 0, 0))],
        out_specs=row(D_MODEL),
        out_shape=jax.ShapeDtypeStruct((m, D_MODEL), F32),
        compiler_params=_params("parallel"),
        name="merge_heads",
    )(x, o_mla, o_chunk, o_mem, w["out_norm"], w["w_o"])


def _rope_tables(pos):
    inv = ROPE_THETA ** (-jnp.arange(0, QK_ROPE_DIM, 2, dtype=F32) / QK_ROPE_DIM)
    ang = pos.astype(F32)[:, None] * inv[None, :]
    c, s = jnp.cos(ang), jnp.sin(ang)
    z32, z64 = jnp.zeros_like(c), jnp.zeros((pos.shape[0], LANES - QK_ROPE_DIM), F32)
    return (jnp.concatenate([c, c, z64], axis=-1),
            jnp.concatenate([-s, z32, z64], axis=-1),
            jnp.concatenate([z32, s, z64], axis=-1))


def _pad_lanes(a, width):
    return jnp.pad(a, [(0, 0)] * (a.ndim - 1) + [(0, width - a.shape[-1])])


def _prepare_weights(w_in, w_uq, w_uk, w_uv, w_spatial, b_spatial, bd, t, gains):
    depth = w_in.shape[0]
    b = np.cumsum([Q_LORA_RANK, KV_LORA_RANK, QK_ROPE_DIM, CHUNK_WIDTH, CHUNK_WIDTH]).tolist()
    c_q, c_kv, k_r, u, v, q_m = (w_in[..., :b[0]], w_in[..., b[0]:b[1]], w_in[..., b[1]:b[2]],
                                 w_in[..., b[2]:b[3]], w_in[..., b[3]:b[4]], w_in[..., b[4]:])
    w = {}
    w["w_in"] = jnp.concatenate([c_q, c_kv, u, v, q_m, _pad_lanes(k_r, LANES)], axis=-1).astype(BF16)
    uq = w_uq.reshape(depth, Q_LORA_RANK, MLA_HEADS, QK_NOPE_DIM + QK_ROPE_DIM)
    w["w_uq"] = jnp.concatenate(
        [uq[..., :QK_NOPE_DIM].reshape(depth, Q_LORA_RANK, MLA_WIDTH),
         _pad_lanes(uq[..., QK_NOPE_DIM:], LANES).reshape(depth, Q_LORA_RANK, MLA_WIDTH)], axis=-1).astype(BF16)
    uk = w_uk.reshape(depth, KV_LORA_RANK, MLA_HEADS * QK_NOPE_DIM)
    w["w_uk_f32"] = uk
    w["w_uk"] = uk.astype(BF16)
    w["w_uk_t"] = jnp.swapaxes(uk, 1, 2).astype(BF16)
    w["w_uv"] = w_uv.reshape(depth, KV_LORA_RANK, MLA_WIDTH).astype(BF16)
    reps = CHUNK // t
    eye = jnp.eye(reps, dtype=F32)
    w_blk = jnp.einsum("ab,lhts->lhatbs", eye, w_spatial[:, :, :t, :t]).reshape(depth, CHUNK_HEADS, CHUNK, CHUNK)
    w["w_sp_prompt"] = w_spatial
    w["w_sp_sample"] = w_blk
    w["b_sp_prompt"] = jnp.swapaxes(b_spatial, 1, 2)
    w["b_sp_sample"] = jnp.tile(jnp.swapaxes(b_spatial[:, :, :t], 1, 2), (1, reps, 1))
    for name, gval in gains.items():
        gval = gval[:, None, :]
        if gval.shape[-1] < LANES:
            gval = _pad_lanes(gval, LANES)
        w[name] = gval
    return w


def kernel(x_prompt, x_sample, mem_prompt, cache_kv_latent, cache_k_rope, cache_mem_k, cache_mem_v, page_table, ffn1_norm, ffn1_w_gate, ffn1_w_up, ffn1_w_down, mix_norm, w_in, q_lat_norm, w_uq, q_nope_norm, q_rope_norm, kv_lat_norm, w_uk, w_uv, k_nope_norm, k_rope_norm, sgu_norm, w_spatial, b_spatial, mem_norm, w_mem_kv, mem_q_norm, mem_k_norm, out_norm, w_o, ffn2_norm, ffn2_w_gate, ffn2_w_up, ffn2_w_down):
    batch, seq, _ = x_prompt.shape
    bd, t, _ = x_sample.shape
    depth = w_in.shape[0]
    n_pages = page_table.shape[1]
    past_len = n_pages * PAGE_SIZE
    assert (bd * t) % CHUNK == 0 and CHUNK % t == 0 and seq % CHUNK == 0

    gains = dict(mix_norm=mix_norm, q_lat_norm=q_lat_norm, q_nope_norm=q_nope_norm, q_rope_norm=q_rope_norm,
                 kv_lat_norm=kv_lat_norm, k_nope_norm=k_nope_norm, k_rope_norm=k_rope_norm, sgu_norm=sgu_norm,
                 mem_norm=mem_norm, mem_q_norm=mem_q_norm, mem_k_norm=mem_k_norm, out_norm=out_norm,
                 ffn1_norm=ffn1_norm, ffn2_norm=ffn2_norm)
    w = _prepare_weights(w_in, w_uq, w_uk, w_uv, w_spatial, b_spatial, bd, t, gains)
    w["w_mem_kv"] = w_mem_kv.astype(BF16)
    w["w_o"] = w_o.astype(BF16)
    ffn_w = [(w["ffn1_norm"], ffn1_w_gate.astype(BF16), ffn1_w_up.astype(BF16), ffn1_w_down.astype(BF16)),
             (w["ffn2_norm"], ffn2_w_gate.astype(BF16), ffn2_w_up.astype(BF16), ffn2_w_down.astype(BF16))]

    tabs_p = _rope_tables(jnp.tile(jnp.arange(seq, dtype=jnp.int32), batch))
    tabs_s = _rope_tables(jnp.tile(past_len + jnp.arange(t, dtype=jnp.int32), bd))
    pt_flat = page_table.reshape(-1)
    cache_mk = cache_mem_k.reshape(depth, bd, N_MEM * MEM_HEADS, MEM_HEAD_DIM)
    cache_mv = cache_mem_v.reshape(depth, bd, N_MEM * MEM_HEADS, MEM_HEAD_DIM)
    cache_kr_t = jnp.swapaxes(cache_k_rope, 2, 3)

    xp = x_prompt.reshape(batch * seq, D_MODEL)
    xs = x_sample.reshape(bd * t, D_MODEL)
    mem = mem_prompt.reshape(batch * N_MEM, D_MODEL)
    outs = [[] for _ in range(7)]
    for l in range(depth):
        xs = _ffn(xs, *ffn_w[0], l)
        q, lat, kr, u, v, qm = _mixer(xs, tabs_s, w, l, False)
        qa, qr = _absorb(q, w, l, bd, t)
        o_lat = _mla_sample(pt_flat, qa, qr, lat, kr, cache_kv_latent, cache_kr_t, w, l, bd, t, n_pages)
        o_mla = _uv_up(o_lat, w, l, bd, t)
        o_chunk = _chunk_mlp(u, v, w["w_sp_sample"], w["b_sp_sample"], l)
        o_mem = _mem_attn_sample(qm, cache_mk, cache_mv, l, bd, t)
        xs = _merge(xs, o_mla, o_chunk, o_mem, w, l)
        outs[4].append(lat.reshape(bd, t, KV_LORA_RANK))
        outs[5].append(kr[:, :QK_ROPE_DIM].reshape(bd, t, QK_ROPE_DIM))
        outs[6].append(v.reshape(bd, t, CHUNK_WIDTH))
        xs = _ffn(xs, *ffn_w[1], l)
        xp = _ffn(xp, *ffn_w[0], l)
        q, lat, kr, u, v, qm, k, vv = _mixer(xp, tabs_p, w, l, True)
        mk, mv = _mem_kv(mem, w, l)
        o_mla = _mla_prompt(q, k, vv, batch, seq)
        o_chunk = _chunk_mlp(u, v, w["w_sp_prompt"], w["b_sp_prompt"], l)
        o_mem = _mem_attn_prompt(qm, mk, mv, batch, seq)
        xp = _merge(xp, o_mla, o_chunk, o_mem, w, l)
        outs[0].append(lat.reshape(batch, seq, KV_LORA_RANK))
        outs[1].append(kr[:, :QK_ROPE_DIM].reshape(batch, seq, QK_ROPE_DIM))
        outs[2].append(mk.reshape(batch, N_MEM, MEM_HEADS, MEM_HEAD_DIM))
        outs[3].append(mv.reshape(batch, N_MEM, MEM_HEADS, MEM_HEAD_DIM))
        xp = _ffn(xp, *ffn_w[1], l)
    return (xp.reshape(batch, seq, D_MODEL), xs.reshape(bd, t, D_MODEL), *[jnp.stack(o) for o in outs])
```

```python
import functools

import numpy as np
import jax
import jax.numpy as jnp
from jax import lax
from jax.experimental import pallas as pl
from jax.experimental.pallas import tpu as pltpu

F32 = jnp.float32
BF16 = jnp.bfloat16

D_MODEL = 2048
D_FF = 5632
MLA_HEADS = 8
QK_NOPE_DIM = 128
QK_ROPE_DIM = 64
V_HEAD_DIM = 128
Q_LORA_RANK = 512
KV_LORA_RANK = 256
MLA_WIDTH = MLA_HEADS * V_HEAD_DIM
MLA_SCALE = (QK_NOPE_DIM + QK_ROPE_DIM) ** -0.5
ROPE_THETA = 10000.0
PAGE_SIZE = 128
CHUNK = 128
CHUNK_HEADS = 4
CHUNK_WIDTH = 512
N_MEM = 256
MEM_HEADS = 4
MEM_HEAD_DIM = 128
MEM_WIDTH = 512
MEM_SCALE = MEM_HEAD_DIM ** -0.5
EPS = 1e-6
LOG2E = 1.4426950408889634

LANES = 128
QK_PAD = 2 * LANES
IN_PAD_WIDTH = Q_LORA_RANK + KV_LORA_RANK + 2 * CHUNK_WIDTH + MEM_WIDTH + LANES
VMEM_LIMIT = 56 * 1024 * 1024

_O_CQ, _O_KV, _O_U, _O_V, _O_QM, _O_KR = 0, 512, 768, 1280, 1792, 2304


def _params(*sem):
    return pltpu.CompilerParams(dimension_semantics=sem, vmem_limit_bytes=VMEM_LIMIT)


def _pick(n, cands):
    for c in cands:
        if n % c == 0:
            return c
    raise ValueError(f"no tile for {n}")


def _rms(x, g):
    ms = jnp.mean(x * x, axis=-1, keepdims=True)
    return x * lax.rsqrt(ms + EPS) * g


def _rms_pad(x, g, n):
    ms = jnp.sum(x * x, axis=-1, keepdims=True) * (1.0 / n)
    return x * lax.rsqrt(ms + EPS) * g


def _gelu(x):
    c = np.sqrt(2.0 / np.pi).astype(np.float32)
    return x * (0.5 * (1.0 + jnp.tanh(c * (x + 0.044715 * (x * x * x)))))


def _dot(a, b):
    return jnp.dot(a, b, preferred_element_type=F32)


def _dot_nt(a, b):
    return lax.dot_general(a, b, (((1,), (1,)), ((), ())), preferred_element_type=F32)


def _ffn_kernel(x_ref, g_ref, wg_ref, wu_ref, wd_ref, o_ref, h_ref):
    f = pl.program_id(1)

    @pl.when(f == 0)
    def _():
        h_ref[...] = _rms(x_ref[...], g_ref[...]).astype(BF16)
        o_ref[...] = jnp.zeros_like(o_ref)

    h = h_ref[...]
    a = _dot(h, wg_ref[...])
    b = _dot(h, wu_ref[...])
    act = (a * (1.0 / (1.0 + jnp.exp(-a))) * b).astype(BF16)
    o_ref[...] += _dot(act, wd_ref[...])

    @pl.when(f == pl.num_programs(1) - 1)
    def _():
        o_ref[...] = x_ref[...] + 0.5 * o_ref[...]


def _ffn(x, g, wg, wu, wd, layer):
    m = x.shape[0]
    tm = _pick(m, (1024, 512, 256, 128))
    tf = 256
    return pl.pallas_call(
        _ffn_kernel,
        grid=(m // tm, D_FF // tf),
        in_specs=[
            pl.BlockSpec((tm, D_MODEL), lambda i, f: (i, 0)),
            pl.BlockSpec((None, 1, D_MODEL), lambda i, f: (layer, 0, 0)),
            pl.BlockSpec((None, D_MODEL, tf), lambda i, f: (layer, 0, f)),
            pl.BlockSpec((None, D_MODEL, tf), lambda i, f: (layer, 0, f)),
            pl.BlockSpec((None, tf, D_MODEL), lambda i, f: (layer, f, 0)),
        ],
        out_specs=pl.BlockSpec((tm, D_MODEL), lambda i, f: (i, 0)),
        out_shape=jax.ShapeDtypeStruct((m, D_MODEL), F32),
        scratch_shapes=[pltpu.VMEM((tm, D_MODEL), BF16)],
        compiler_params=_params("parallel", "arbitrary"),
        name="ffn",
    )(x, g, wg, wu, wd)


def _mixer_kernel(x_ref, c_ref, s1_ref, s2_ref, gmix_ref, win_ref, gqlat_ref, wuq_ref, gqn_ref, gqr_ref,
                  gkv_ref, gkr_ref, gsgu_ref, gmq_ref, *rest, with_kv, sub_rows):
    if with_kv:
        wuk_ref, gkn_ref, wuv_ref, q_ref, lat_ref, kr_ref, u_ref, v_ref, qm_ref, k_ref, vv_ref = rest
    else:
        q_ref, lat_ref, kr_ref, u_ref, v_ref, qm_ref = rest

    for r0 in range(0, x_ref.shape[0], sub_rows):
        rs = slice(r0, r0 + sub_rows)
        h = _rms(x_ref[rs, :], gmix_ref[...]).astype(BF16)
        z = _dot(h, win_ref[...])
        cos, sin_lo, sin_hi = c_ref[rs, :], s1_ref[rs, :], s2_ref[rs, :]

        def rope(t):
            return t * cos + pltpu.roll(t, LANES - 32, 1) * sin_lo + pltpu.roll(t, 32, 1) * sin_hi

        cq = _rms(z[:, _O_CQ:_O_CQ + Q_LORA_RANK], gqlat_ref[...]).astype(BF16)
        q = _dot(cq, wuq_ref[...])
        for hd in range(MLA_HEADS):
            qn = _rms(q[:, hd * LANES:(hd + 1) * LANES], gqn_ref[...])
            qr = rope(_rms_pad(q[:, MLA_WIDTH + hd * LANES:MLA_WIDTH + (hd + 1) * LANES], gqr_ref[...], QK_ROPE_DIM))
            q_ref[rs, hd * QK_PAD:hd * QK_PAD + LANES] = qn.astype(BF16)
            q_ref[rs, hd * QK_PAD + LANES:(hd + 1) * QK_PAD] = qr.astype(BF16)

        lat = _rms(z[:, _O_KV:_O_KV + KV_LORA_RANK], gkv_ref[...])
        lat_ref[rs, :] = lat
        kr = rope(_rms_pad(z[:, _O_KR:_O_KR + LANES], gkr_ref[...], QK_ROPE_DIM))
        kr_ref[rs, :] = kr
        u_ref[rs, :] = _gelu(z[:, _O_U:_O_U + CHUNK_WIDTH])
        v_ref[rs, :] = _rms(_gelu(z[:, _O_V:_O_V + CHUNK_WIDTH]), gsgu_ref[...])
        for hd in range(MEM_HEADS):
            sl = slice(hd * LANES, (hd + 1) * LANES)
            qm_ref[rs, sl] = _rms(z[:, _O_QM + hd * LANES:_O_QM + (hd + 1) * LANES], gmq_ref[...])

        if with_kv:
            latb = lat.astype(BF16)
            kn = _dot(latb, wuk_ref[...])
            krb = kr.astype(BF16)
            for hd in range(MLA_HEADS):
                k_ref[rs, hd * QK_PAD:hd * QK_PAD + LANES] = _rms(kn[:, hd * LANES:(hd + 1) * LANES], gkn_ref[...]).astype(BF16)
                k_ref[rs, hd * QK_PAD + LANES:(hd + 1) * QK_PAD] = krb
            vv_ref[rs, :] = _dot(latb, wuv_ref[...]).astype(BF16)


def _mixer(x, tabs, w, layer, with_kv):
    m = x.shape[0]
    tm = _pick(m, (512, 256, 128))
    sub_rows = min(tm, 256)
    row = lambda n: pl.BlockSpec((tm, n), lambda i: (i, 0))
    gain = lambda n: pl.BlockSpec((None, 1, n), lambda i: (layer, 0, 0))
    wspec = lambda r, c: pl.BlockSpec((None, r, c), lambda i: (layer, 0, 0), pipeline_mode=pl.Buffered(1))
    in_specs = [row(D_MODEL), row(LANES), row(LANES), row(LANES), gain(D_MODEL), wspec(D_MODEL, IN_PAD_WIDTH),
                gain(Q_LORA_RANK), wspec(Q_LORA_RANK, 2 * MLA_WIDTH), gain(LANES), gain(LANES),
                gain(KV_LORA_RANK), gain(LANES), gain(CHUNK_WIDTH), gain(LANES)]
    args = [x, *tabs, w["mix_norm"], w["w_in"], w["q_lat_norm"], w["w_uq"], w["q_nope_norm"], w["q_rope_norm"],
            w["kv_lat_norm"], w["k_rope_norm"], w["sgu_norm"], w["mem_q_norm"]]
    out_specs = [row(MLA_HEADS * QK_PAD), row(KV_LORA_RANK), row(LANES), row(CHUNK_WIDTH), row(CHUNK_WIDTH), row(MEM_WIDTH)]
    out_shape = [jax.ShapeDtypeStruct((m, MLA_HEADS * QK_PAD), BF16), jax.ShapeDtypeStruct((m, KV_LORA_RANK), F32),
                 jax.ShapeDtypeStruct((m, LANES), F32), jax.ShapeDtypeStruct((m, CHUNK_WIDTH), F32),
                 jax.ShapeDtypeStruct((m, CHUNK_WIDTH), F32), jax.ShapeDtypeStruct((m, MEM_WIDTH), F32)]
    if with_kv:
        in_specs += [wspec(KV_LORA_RANK, MLA_WIDTH), gain(LANES), wspec(KV_LORA_RANK, MLA_WIDTH)]
        args += [w["w_uk"], w["k_nope_norm"], w["w_uv"]]
        out_specs += [row(MLA_HEADS * QK_PAD), row(MLA_WIDTH)]
        out_shape += [jax.ShapeDtypeStruct((m, MLA_HEADS * QK_PAD), BF16), jax.ShapeDtypeStruct((m, MLA_WIDTH), BF16)]
    return pl.pallas_call(
        functools.partial(_mixer_kernel, with_kv=with_kv, sub_rows=sub_rows),
        grid=(m // tm,),
        in_specs=in_specs, out_specs=out_specs, out_shape=out_shape,
        compiler_params=_params("parallel"),
        name="mixer_kv" if with_kv else "mixer",
    )(*args)


def _mla_prompt_kernel(q_ref, k_ref, v_ref, o_ref, *, seq, tq):
    rows = lax.broadcasted_iota(jnp.int32, (tq, tq), 0)
    cols = lax.broadcasted_iota(jnp.int32, (tq, tq), 1)
    causal = cols <= rows
    for qi in range(seq // tq):
        q = q_ref[qi * tq:(qi + 1) * tq, :]
        m = jnp.full((tq, 1), -1e30, F32)
        l = jnp.zeros((tq, 1), F32)
        acc = jnp.zeros((tq, V_HEAD_DIM), F32)
        for kb in range(qi + 1):
            s = _dot_nt(q, k_ref[kb * tq:(kb + 1) * tq, :]) * (MLA_SCALE * LOG2E)
            if kb == qi:
                s = jnp.where(causal, s, -jnp.inf)
            m_new = jnp.maximum(m, jnp.max(s, axis=-1, keepdims=True))
            p = jnp.exp2(s - m_new)
            corr = jnp.exp2(m - m_new)
            l = l * corr + jnp.sum(p, axis=-1, keepdims=True)
            acc = acc * corr + _dot(p.astype(BF16), v_ref[kb * tq:(kb + 1) * tq, :])
            m = m_new
        o_ref[qi * tq:(qi + 1) * tq, :] = acc / l


def _mla_prompt(q, k, v, batch, seq):
    tq = _pick(seq, (512, 256, 128))
    return pl.pallas_call(
        functools.partial(_mla_prompt_kernel, seq=seq, tq=tq),
        grid=(batch, MLA_HEADS),
        in_specs=[
            pl.BlockSpec((seq, QK_PAD), lambda b, h: (b, h)),
            pl.BlockSpec((seq, QK_PAD), lambda b, h: (b, h)),
            pl.BlockSpec((seq, V_HEAD_DIM), lambda b, h: (b, h)),
        ],
        out_specs=pl.BlockSpec((seq, V_HEAD_DIM), lambda b, h: (b, h)),
        out_shape=jax.ShapeDtypeStruct((batch * seq, MLA_WIDTH), F32),
        compiler_params=_params("parallel", "parallel"),
        name="mla_prompt",
    )(q, k, v)


def _absorb_kernel(qn_ref, qr_ref, wuk_ref, g_ref, qa_ref, qro_ref, *, bd, t):
    wg = (wuk_ref[...] * g_ref[...]).astype(BF16)
    qa = _dot_nt(qn_ref[...], wg)
    qa_ref[...] = qa.reshape(bd, t, KV_LORA_RANK)
    qro_ref[...] = qr_ref[...].astype(F32).reshape(bd, t, LANES)


def _absorb(q, w, layer, bd, t):
    ms = bd * t
    return pl.pallas_call(
        functools.partial(_absorb_kernel, bd=bd, t=t),
        grid=(MLA_HEADS,),
        in_specs=[
            pl.BlockSpec((ms, LANES), lambda h: (0, 2 * h)),
            pl.BlockSpec((ms, LANES), lambda h: (0, 2 * h + 1)),
            pl.BlockSpec((None, KV_LORA_RANK, LANES), lambda h: (layer, 0, h)),
            pl.BlockSpec((None, 1, LANES), lambda h: (layer, 0, 0)),
        ],
        out_specs=[pl.BlockSpec((bd, t, KV_LORA_RANK), lambda h: (0, h, 0)),
                   pl.BlockSpec((bd, t, LANES), lambda h: (0, h, 0))],
        out_shape=[jax.ShapeDtypeStruct((bd, MLA_HEADS * t, KV_LORA_RANK), F32),
                   jax.ShapeDtypeStruct((bd, MLA_HEADS * t, LANES), F32)],
        compiler_params=_params("parallel"),
        name="mla_absorb",
    )(q, q, w["w_uk_f32"], w["k_nope_norm"])


def _mla_sample_kernel(pt_ref, qa_ref, qr_ref, wukt_ref, latn_ref, krn_ref, clat_hbm, ckr_hbm, o_ref,
                       lat_buf, kr_buf, sem, a_sc, latb_sc, new_lat, new_kr, s_sc, *, layer, n_pages, t, ts, group):
    b = pl.program_id(0)
    slot = b % 2
    nq = MLA_HEADS * t
    nk = MLA_HEADS * QK_NOPE_DIM
    n_groups = n_pages * PAGE_SIZE // (ts * group)
    pages_per_sub = ts // PAGE_SIZE

    def page_copies(bb, sl, j):
        pg = pt_ref[bb * n_pages + j]
        row0 = pl.multiple_of(j * PAGE_SIZE, PAGE_SIZE)
        return (pltpu.make_async_copy(clat_hbm.at[layer, pg], lat_buf.at[sl, pl.ds(row0, PAGE_SIZE)], sem.at[0, sl]),
                pltpu.make_async_copy(ckr_hbm.at[layer, pg], kr_buf.at[sl, j], sem.at[1, sl]))

    @pl.when(b == 0)
    def _():
        a_sc[0:nk, :] = wukt_ref[...]

        def body(j, c):
            for cp in page_copies(0, 0, j):
                cp.start()
            return c
        lax.fori_loop(0, n_pages, body, 0)

    nxt = (b + 1) % pl.num_programs(0)

    def wait_body(j, c):
        for cp in page_copies(b, slot, j):
            cp.wait()
        return c
    lax.fori_loop(0, n_pages, wait_body, 0, unroll=_pick(n_pages, (8, 4, 2, 1)))

    a_sc[nk:nk + nq, :] = qa_ref[0].astype(BF16)
    qr = qr_ref[0][:, 0:QK_ROPE_DIM].astype(BF16)

    def nope_scores(latb):
        st = _dot_nt(a_sc[...], latb)
        out = []
        for hd in range(MLA_HEADS):
            kn = st[hd * QK_NOPE_DIM:(hd + 1) * QK_NOPE_DIM, :]
            rn = lax.rsqrt(jnp.mean(kn * kn, axis=0, keepdims=True) + EPS)
            out.append(st[nk + hd * t:nk + (hd + 1) * t, :] * rn)
        return jnp.concatenate(out, axis=0)

    def lane_tile_max(s):
        out = s[:, 0:LANES]
        for c in range(1, s.shape[1] // LANES):
            out = jnp.maximum(out, s[:, c * LANES:(c + 1) * LANES])
        return out

    def lane_tile_sum(p):
        out = p[:, 0:LANES]
        for c in range(1, p.shape[1] // LANES):
            out = out + p[:, c * LANES:(c + 1) * LANES]
        return out

    def score_group(g):
        gmax = None
        for k in range(group):
            i = g * group + k
            for jj in range(pages_per_sub):
                for cp in page_copies(nxt, 1 - slot, i * pages_per_sub + jj):
                    cp.start()
            row0 = pl.multiple_of(i * ts, ts)
            latb = lat_buf[slot, pl.ds(row0, ts), :].astype(BF16)
            latb_sc[pl.ds(row0, ts), :] = latb
            krt = jnp.concatenate(
                [kr_buf[slot, i * pages_per_sub + jj].astype(BF16) for jj in range(pages_per_sub)], axis=1)
            s = (nope_scores(latb) + _dot(qr, krt)) * (MLA_SCALE * LOG2E)
            s_sc[i] = s
            tmax = lane_tile_max(s)
            gmax = tmax if gmax is None else jnp.maximum(gmax, tmax)
        return jnp.max(gmax, axis=-1, keepdims=True)

    def merge(state, smax, tiles):
        m, acc, psum = state
        m_new = jnp.maximum(m, smax)
        corr = jnp.exp2(m - m_new)
        acc = acc * corr
        psum = psum * corr
        for s, latb in tiles:
            p = jnp.exp2(s - m_new)
            psum = psum + lane_tile_sum(p)
            acc = acc + _dot(p.astype(BF16), latb)
        return m_new, acc, psum

    def merge_group(state, smax, g):
        tiles = []
        for k in range(group):
            i = g * group + k
            tiles.append((s_sc[i], latb_sc[pl.ds(pl.multiple_of(i * ts, ts), ts), :]))
        return merge(state, smax, tiles)

    state = (jnp.full((nq, 1), -1e30, F32), jnp.zeros((nq, KV_LORA_RANK), F32), jnp.zeros((nq, LANES), F32))

    smax_prev = score_group(0)
    for g in range(1, n_groups):
        smax = score_group(g)
        state = merge_group(state, smax_prev, g - 1)
        smax_prev = smax
    state = merge_group(state, smax_prev, n_groups - 1)

    @pl.when(b == pl.num_programs(0) - 1)
    def _():
        def body(j, c):
            for cp in page_copies(nxt, 1 - slot, j):
                cp.wait()
            return c
        lax.fori_loop(0, n_pages, body, 0)

    new_lat[...] = jnp.zeros_like(new_lat)
    new_kr[...] = jnp.zeros_like(new_kr)
    new_lat[0:t, :] = latn_ref[...]
    new_kr[0:t, :] = krn_ref[:, 0:QK_ROPE_DIM]
    newl = new_lat[...].astype(BF16)
    s_new = (nope_scores(newl) + _dot_nt(qr, new_kr[...].astype(BF16))) * (MLA_SCALE * LOG2E)
    rows = lax.broadcasted_iota(jnp.int32, (nq, PAGE_SIZE), 0)
    cols = lax.broadcasted_iota(jnp.int32, (nq, PAGE_SIZE), 1)
    s_new = jnp.where(cols <= rows % t, s_new, -jnp.inf)
    _, acc, psum = merge(state, jnp.max(s_new, axis=-1, keepdims=True), [(s_new, newl)])
    o_ref[0] = acc / jnp.sum(psum, axis=-1, keepdims=True)


def _mla_sample(pt_flat, qa, qr, lat_new, kr_new, cache_lat, cache_kr_t, w, layer, bd, t, n_pages):
    nq = MLA_HEADS * t
    tk = n_pages * PAGE_SIZE
    ts = _pick(tk, (512, 256, 128))
    group = _pick(tk // ts, (4, 2, 1))
    grid_spec = pltpu.PrefetchScalarGridSpec(
        num_scalar_prefetch=1,
        grid=(bd,),
        in_specs=[
            pl.BlockSpec((1, nq, KV_LORA_RANK), lambda b, pt: (b, 0, 0)),
            pl.BlockSpec((1, nq, LANES), lambda b, pt: (b, 0, 0)),
            pl.BlockSpec((None, MLA_HEADS * QK_NOPE_DIM, KV_LORA_RANK), lambda b, pt: (layer, 0, 0)),
            pl.BlockSpec((t, KV_LORA_RANK), lambda b, pt: (b, 0)),
            pl.BlockSpec((t, LANES), lambda b, pt: (b, 0)),
            pl.BlockSpec(memory_space=pl.ANY),
            pl.BlockSpec(memory_space=pl.ANY),
        ],
        out_specs=pl.BlockSpec((1, nq, KV_LORA_RANK), lambda b, pt: (b, 0, 0)),
        scratch_shapes=[
            pltpu.VMEM((2, tk, KV_LORA_RANK), F32),
            pltpu.VMEM((2, n_pages, QK_ROPE_DIM, PAGE_SIZE), F32),
            pltpu.SemaphoreType.DMA((2, 2)),
            pltpu.VMEM((MLA_HEADS * QK_NOPE_DIM + nq, KV_LORA_RANK), BF16),
            pltpu.VMEM((tk, KV_LORA_RANK), BF16),
            pltpu.VMEM((PAGE_SIZE, KV_LORA_RANK), F32),
            pltpu.VMEM((PAGE_SIZE, QK_ROPE_DIM), F32),
            pltpu.VMEM((tk // ts, nq, ts), F32),
        ],
    )
    return pl.pallas_call(
        functools.partial(_mla_sample_kernel, layer=layer, n_pages=n_pages, t=t, ts=ts, group=group),
        grid_spec=grid_spec,
        out_shape=jax.ShapeDtypeStruct((bd, nq, KV_LORA_RANK), F32),
        compiler_params=_params("arbitrary"),
        name="mla_sample",
    )(pt_flat, qa, qr, w["w_uk_t"], lat_new, kr_new, cache_lat, cache_kr_t)


def _uv_up_kernel(o_ref, wuv_ref, out_ref, *, bd, t):
    o = o_ref[...].reshape(bd * t, KV_LORA_RANK).astype(BF16)
    out_ref[...] = _dot(o, wuv_ref[...])


def _uv_up(o_lat, w, layer, bd, t):
    return pl.pallas_call(
        functools.partial(_uv_up_kernel, bd=bd, t=t),
        grid=(MLA_HEADS,),
        in_specs=[pl.BlockSpec((bd, t, KV_LORA_RANK), lambda h: (0, h, 0)),
                  pl.BlockSpec((None, KV_LORA_RANK, V_HEAD_DIM), lambda h: (layer, 0, h))],
        out_specs=pl.BlockSpec((bd * t, V_HEAD_DIM), lambda h: (0, h)),
        out_shape=jax.ShapeDtypeStruct((bd * t, MLA_WIDTH), F32),
        compiler_params=_params("parallel"),
        name="mla_uv_up",
    )(o_lat, w["w_uv"])


def _chunk_kernel(u_ref, v_ref, w_ref, b_ref, o_ref):
    rows = lax.broadcasted_iota(jnp.int32, (CHUNK, CHUNK), 0)
    cols = lax.broadcasted_iota(jnp.int32, (CHUNK, CHUNK), 1)
    causal = cols <= rows
    for hd in range(CHUNK_HEADS):
        sl = slice(hd * LANES, (hd + 1) * LANES)
        wm = jnp.where(causal, w_ref[hd], 0.0).astype(BF16)
        bias = b_ref[:, hd:hd + 1]
        for r0 in range(0, u_ref.shape[0], CHUNK):
            rs = slice(r0, r0 + CHUNK)
            s = _dot(wm, v_ref[rs, sl].astype(BF16)) + bias
            o_ref[rs, sl] = u_ref[rs, sl] * s


def _chunk_mlp(u, v, w_sp, b_sp, layer):
    m = u.shape[0]
    tm = _pick(m, (512, 256, 128))
    return pl.pallas_call(
        _chunk_kernel,
        grid=(m // tm,),
        in_specs=[pl.BlockSpec((tm, CHUNK_WIDTH), lambda c: (c, 0)),
                  pl.BlockSpec((tm, CHUNK_WIDTH), lambda c: (c, 0)),
                  pl.BlockSpec((None, CHUNK_HEADS, CHUNK, CHUNK), lambda c: (layer, 0, 0, 0)),
                  pl.BlockSpec((None, CHUNK, CHUNK_HEADS), lambda c: (layer, 0, 0))],
        out_specs=pl.BlockSpec((tm, CHUNK_WIDTH), lambda c: (c, 0)),
        out_shape=jax.ShapeDtypeStruct((m, CHUNK_WIDTH), F32),
        compiler_params=_params("parallel"),
        name="chunk_mlp",
    )(u, v, w_sp, b_sp)


def _mem_kv_kernel(x_ref, g_ref, w_ref, gk_ref, k_ref, v_ref):
    h = _rms(x_ref[...], g_ref[...]).astype(BF16)
    kv = _dot(h, w_ref[...])
    for hd in range(MEM_HEADS):
        sl = slice(hd * LANES, (hd + 1) * LANES)
        k_ref[:, sl] = _rms(kv[:, sl], gk_ref[...])
    v_ref[...] = kv[:, MEM_WIDTH:]


def _mem_kv(mem, w, layer):
    m = mem.shape[0]
    tm = _pick(m, (256, 128))
    return pl.pallas_call(
        _mem_kv_kernel,
        grid=(m // tm,),
        in_specs=[pl.BlockSpec((tm, D_MODEL), lambda i: (i, 0)),
                  pl.BlockSpec((None, 1, D_MODEL), lambda i: (layer, 0, 0)),
                  pl.BlockSpec((None, D_MODEL, 2 * MEM_WIDTH), lambda i: (layer, 0, 0)),
                  pl.BlockSpec((None, 1, LANES), lambda i: (layer, 0, 0))],
        out_specs=[pl.BlockSpec((tm, MEM_WIDTH), lambda i: (i, 0))] * 2,
        out_shape=[jax.ShapeDtypeStruct((m, MEM_WIDTH), F32)] * 2,
        compiler_params=_params("parallel"),
        name="mem_kv",
    )(mem, w["mem_norm"], w["w_mem_kv"], w["mem_k_norm"])


def _softmax_rows(s):
    e = jnp.exp(s - jnp.max(s, axis=-1, keepdims=True))
    return e / jnp.sum(e, axis=-1, keepdims=True)


def _mem_attn_prompt_kernel(q_ref, k_ref, v_ref, o_ref):
    for hd in range(MEM_HEADS):
        sl = slice(hd * LANES, (hd + 1) * LANES)
        s = _dot_nt(q_ref[:, sl].astype(BF16), k_ref[:, sl].astype(BF16)) * MEM_SCALE
        o_ref[:, sl] = _dot(_softmax_rows(s).astype(BF16), v_ref[:, sl].astype(BF16))


def _mem_attn_prompt(qm, mk, mv, batch, seq):
    tq = _pick(seq, (1024, 512, 256, 128))
    nq = seq // tq
    return pl.pallas_call(
        _mem_attn_prompt_kernel,
        grid=(batch, nq),
        in_specs=[pl.BlockSpec((tq, MEM_WIDTH), lambda b, i: (b * nq + i, 0)),
                  pl.BlockSpec((N_MEM, MEM_WIDTH), lambda b, i: (b, 0)),
                  pl.BlockSpec((N_MEM, MEM_WIDTH), lambda b, i: (b, 0))],
        out_specs=pl.BlockSpec((tq, MEM_WIDTH), lambda b, i: (b * nq + i, 0)),
        out_shape=jax.ShapeDtypeStruct((batch * seq, MEM_WIDTH), F32),
        compiler_params=_params("parallel", "parallel"),
        name="mem_attn_prompt",
    )(qm, mk, mv)


def _mem_attn_sample_kernel(q_ref, k_ref, v_ref, o_ref, *, bb, t):
    for hd in range(MEM_HEADS):
        sl = slice(hd * LANES, (hd + 1) * LANES)
        q3 = q_ref[:, sl].reshape(bb, t, MEM_HEAD_DIM).astype(BF16)
        k3 = k_ref[:, pl.ds(hd, N_MEM, stride=MEM_HEADS), :].astype(BF16)
        v3 = v_ref[:, pl.ds(hd, N_MEM, stride=MEM_HEADS), :].astype(BF16)
        s = jnp.einsum("bqd,bkd->bqk", q3, k3, preferred_element_type=F32) * MEM_SCALE
        o3 = jnp.einsum("bqk,bkd->bqd", _softmax_rows(s).astype(BF16), v3, preferred_element_type=F32)
        o_ref[:, sl] = o3.reshape(bb * t, MEM_HEAD_DIM)


def _mem_attn_sample(qm, cache_k, cache_v, layer, bd, t):
    bb = _pick(bd, (16, 8, 4, 2, 1))
    kv_spec = pl.BlockSpec((None, bb, N_MEM * MEM_HEADS, MEM_HEAD_DIM), lambda i: (layer, i, 0, 0))
    return pl.pallas_call(
        functools.partial(_mem_attn_sample_kernel, bb=bb, t=t),
        grid=(bd // bb,),
        in_specs=[pl.BlockSpec((bb * t, MEM_WIDTH), lambda i: (i, 0)), kv_spec, kv_spec],
        out_specs=pl.BlockSpec((bb * t, MEM_WIDTH), lambda i: (i, 0)),
        out_shape=jax.ShapeDtypeStruct((bd * t, MEM_WIDTH), F32),
        compiler_params=_params("parallel"),
        name="mem_attn_sample",
    )(qm, cache_k, cache_v)


def _merge_kernel(x_ref, a_ref, c_ref, m_ref, g_ref, w_ref, o_ref):
    g = g_ref[...]
    c0, c1 = MLA_WIDTH, MLA_WIDTH + CHUNK_WIDTH
    acc = _dot(_rms(a_ref[...], g[:, 0:c0]).astype(BF16), w_ref[0:c0, :])
    acc += _dot(_rms(c_ref[...], g[:, c0:c1]).astype(BF16), w_ref[c0:c1, :])
    acc += _dot(_rms(m_ref[...], g[:, c1:]).astype(BF16), w_ref[c1:, :])
    o_ref[...] = x_ref[...] + acc


def _merge(x, o_mla, o_chunk, o_mem, w, layer):
    m = x.shape[0]
    tm = _pick(m, (512, 256, 128))
    row = lambda n: pl.BlockSpec((tm, n), lambda i: (i, 0))
    return pl.pallas_call(
        _merge_kernel,
        grid=(m // tm,),
        in_specs=[row(D_MODEL), row(MLA_WIDTH), row(CHUNK_WIDTH), row(MEM_WIDTH),
                  pl.BlockSpec((None, 1, D_MODEL), lambda i: (layer, 0, 0)),
                  pl.BlockSpec((None, D_MODEL, D_MODEL), lambda i: (layer, 0, 0))],
        out_specs=row(D_MODEL),
        out_shape=jax.ShapeDtypeStruct((m, D_MODEL), F32),
        compiler_params=_params("parallel"),
        name="merge_heads",
    )(x, o_mla, o_chunk, o_mem, w["out_norm"], w["w_o"])


def _rope_tables(pos):
    inv = ROPE_THETA ** (-jnp.arange(0, QK_ROPE_DIM, 2, dtype=F32) / QK_ROPE_DIM)
    ang = pos.astype(F32)[:, None] * inv[None, :]
    c, s = jnp.cos(ang), jnp.sin(ang)
    z32, z64 = jnp.zeros_like(c), jnp.zeros((pos.shape[0], LANES - QK_ROPE_DIM), F32)
    return (jnp.concatenate([c, c, z64], axis=-1),
            jnp.concatenate([-s, z32, z64], axis=-1),
            jnp.concatenate([z32, s, z64], axis=-1))


def _pad_lanes(a, width):
    return jnp.pad(a, [(0, 0)] * (a.ndim - 1) + [(0, width - a.shape[-1])])


def _prepare_weights(w_in, w_uq, w_uk, w_uv, w_spatial, b_spatial, bd, t, gains):
    depth = w_in.shape[0]
    b = np.cumsum([Q_LORA_RANK, KV_LORA_RANK, QK_ROPE_DIM, CHUNK_WIDTH, CHUNK_WIDTH]).tolist()
    c_q, c_kv, k_r, u, v, q_m = (w_in[..., :b[0]], w_in[..., b[0]:b[1]], w_in[..., b[1]:b[2]],
                                 w_in[..., b[2]:b[3]], w_in[..., b[3]:b[4]], w_in[..., b[4]:])
    w = {}
    w["w_in"] = jnp.concatenate([c_q, c_kv, u, v, q_m, _pad_lanes(k_r, LANES)], axis=-1).astype(BF16)
    uq = w_uq.reshape(depth, Q_LORA_RANK, MLA_HEADS, QK_NOPE_DIM + QK_ROPE_DIM)
    w["w_uq"] = jnp.concatenate(
        [uq[..., :QK_NOPE_DIM].reshape(depth, Q_LORA_RANK, MLA_WIDTH),
         _pad_lanes(uq[..., QK_NOPE_DIM:], LANES).reshape(depth, Q_LORA_RANK, MLA_WIDTH)], axis=-1).astype(BF16)
    uk = w_uk.reshape(depth, KV_LORA_RANK, MLA_HEADS * QK_NOPE_DIM)
    w["w_uk_f32"] = uk
    w["w_uk"] = uk.astype(BF16)
    w["w_uk_t"] = jnp.swapaxes(uk, 1, 2).astype(BF16)
    w["w_uv"] = w_uv.reshape(depth, KV_LORA_RANK, MLA_WIDTH).astype(BF16)
    reps = CHUNK // t
    eye = jnp.eye(reps, dtype=F32)
    w_blk = jnp.einsum("ab,lhts->lhatbs", eye, w_spatial[:, :, :t, :t]).reshape(depth, CHUNK_HEADS, CHUNK, CHUNK)
    w["w_sp_prompt"] = w_spatial
    w["w_sp_sample"] = w_blk
    w["b_sp_prompt"] = jnp.swapaxes(b_spatial, 1, 2)
    w["b_sp_sample"] = jnp.tile(jnp.swapaxes(b_spatial[:, :, :t], 1, 2), (1, reps, 1))
    for name, gval in gains.items():
        gval = gval[:, None, :]
        if gval.shape[-1] < LANES:
            gval = _pad_lanes(gval, LANES)
        w[name] = gval
    return w


def kernel(x_prompt, x_sample, mem_prompt, cache_kv_latent, cache_k_rope, cache_mem_k, cache_mem_v, page_table, ffn1_norm, ffn1_w_gate, ffn1_w_up, ffn1_w_down, mix_norm, w_in, q_lat_norm, w_uq, q_nope_norm, q_rope_norm, kv_lat_norm, w_uk, w_uv, k_nope_norm, k_rope_norm, sgu_norm, w_spatial, b_spatial, mem_norm, w_mem_kv, mem_q_norm, mem_k_norm, out_norm, w_o, ffn2_norm, ffn2_w_gate, ffn2_w_up, ffn2_w_down):
    batch, seq, _ = x_prompt.shape
    bd, t, _ = x_sample.shape
    depth = w_in.shape[0]
    n_pages = page_table.shape[1]
    past_len = n_pages * PAGE_SIZE
    assert (bd * t) % CHUNK == 0 and CHUNK % t == 0 and seq % CHUNK == 0

    gains = dict(mix_norm=mix_norm, q_lat_norm=q_lat_norm, q_nope_norm=q_nope_norm, q_rope_norm=q_rope_norm,
                 kv_lat_norm=kv_lat_norm, k_nope_norm=k_nope_norm, k_rope_norm=k_rope_norm, sgu_norm=sgu_norm,
                 mem_norm=mem_norm, mem_q_norm=mem_q_norm, mem_k_norm=mem_k_norm, out_norm=out_norm,
                 ffn1_norm=ffn1_norm, ffn2_norm=ffn2_norm)
    w = _prepare_weights(w_in, w_uq, w_uk, w_uv, w_spatial, b_spatial, bd, t, gains)
    w["w_mem_kv"] = w_mem_kv.astype(BF16)
    w["w_o"] = w_o.astype(BF16)
    ffn_w = [(w["ffn1_norm"], ffn1_w_gate.astype(BF16), ffn1_w_up.astype(BF16), ffn1_w_down.astype(BF16)),
             (w["ffn2_norm"], ffn2_w_gate.astype(BF16), ffn2_w_up.astype(BF16), ffn2_w_down.astype(BF16))]

    tabs_p = _rope_tables(jnp.tile(jnp.arange(seq, dtype=jnp.int32), batch))
    tabs_s = _rope_tables(jnp.tile(past_len + jnp.arange(t, dtype=jnp.int32), bd))
    pt_flat = page_table.reshape(-1)
    cache_mk = cache_mem_k.reshape(depth, bd, N_MEM * MEM_HEADS, MEM_HEAD_DIM)
    cache_mv = cache_mem_v.reshape(depth, bd, N_MEM * MEM_HEADS, MEM_HEAD_DIM)
    cache_kr_t = jnp.swapaxes(cache_k_rope, 2, 3)

    xp = x_prompt.reshape(batch * seq, D_MODEL)
    xs = x_sample.reshape(bd * t, D_MODEL)
    mem = mem_prompt.reshape(batch * N_MEM, D_MODEL)
    outs = [[] for _ in range(7)]
    for l in range(depth):
        xs = _ffn(xs, *ffn_w[0], l)
        q, lat, kr, u, v, qm = _mixer(xs, tabs_s, w, l, False)
        qa, qr = _absorb(q, w, l, bd, t)
        o_lat = _mla_sample(pt_flat, qa, qr, lat, kr, cache_kv_latent, cache_kr_t, w, l, bd, t, n_pages)
        o_mla = _uv_up(o_lat, w, l, bd, t)
        o_chunk = _chunk_mlp(u, v, w["w_sp_sample"], w["b_sp_sample"], l)
        o_mem = _mem_attn_sample(qm, cache_mk, cache_mv, l, bd, t)
        xs = _merge(xs, o_mla, o_chunk, o_mem, w, l)
        outs[4].append(lat.reshape(bd, t, KV_LORA_RANK))
        outs[5].append(kr[:, :QK_ROPE_DIM].reshape(bd, t, QK_ROPE_DIM))
        outs[6].append(v.reshape(bd, t, CHUNK_WIDTH))
        xs = _ffn(xs, *ffn_w[1], l)
        xp = _ffn(xp, *ffn_w[0], l)
        q, lat, kr, u, v, qm, k, vv = _mixer(xp, tabs_p, w, l, True)
        mk, mv = _mem_kv(mem, w, l)
        o_mla = _mla_prompt(q, k, vv, batch, seq)
        o_chunk = _chunk_mlp(u, v, w["w_sp_prompt"], w["b_sp_prompt"], l)
        o_mem = _mem_attn_prompt(qm, mk, mv, batch, seq)
        xp = _merge(xp, o_mla, o_chunk, o_mem, w, l)
        outs[0].append(lat.reshape(batch, seq, KV_LORA_RANK))
        outs[1].append(kr[:, :QK_ROPE_DIM].reshape(batch, seq, QK_ROPE_DIM))
        outs[2].append(mk.reshape(batch, N_MEM, MEM_HEADS, MEM_HEAD_DIM))
        outs[3].append(mv.reshape(batch, N_MEM, MEM_HEADS, MEM_HEAD_DIM))
        xp = _ffn(xp, *ffn_w[1], l)
    return (xp.reshape(batch, seq, D_MODEL), xs.reshape(bd, t, D_MODEL), *[jnp.stack(o) for o in outs])
```

```python
import functools

import numpy as np
import jax
import jax.numpy as jnp
from jax import lax
from jax.experimental import pallas as pl
from jax.experimental.pallas import tpu as pltpu

F32 = jnp.float32
BF16 = jnp.bfloat16

D_MODEL = 2048
D_FF = 5632
MLA_HEADS = 8
QK_NOPE_DIM = 128
QK_ROPE_DIM = 64
V_HEAD_DIM = 128
Q_LORA_RANK = 512
KV_LORA_RANK = 256
MLA_WIDTH = MLA_HEADS * V_HEAD_DIM
MLA_SCALE = (QK_NOPE_DIM + QK_ROPE_DIM) ** -0.5
ROPE_THETA = 10000.0
PAGE_SIZE = 128
CHUNK = 128
CHUNK_HEADS = 4
CHUNK_WIDTH = 512
N_MEM = 256
MEM_HEADS = 4
MEM_HEAD_DIM = 128
MEM_WIDTH = 512
MEM_SCALE = MEM_HEAD_DIM ** -0.5
EPS = 1e-6
LOG2E = 1.4426950408889634

LANES = 128
QK_PAD = 2 * LANES
IN_PAD_WIDTH = Q_LORA_RANK + KV_LORA_RANK + 2 * CHUNK_WIDTH + MEM_WIDTH + LANES
VMEM_LIMIT = 56 * 1024 * 1024

_O_CQ, _O_KV, _O_U, _O_V, _O_QM, _O_KR = 0, 512, 768, 1280, 1792, 2304


def _params(*sem):
    return pltpu.CompilerParams(dimension_semantics=sem, vmem_limit_bytes=VMEM_LIMIT)


def _pick(n, cands):
    for c in cands:
        if n % c == 0:
            return c
    raise ValueError(f"no tile for {n}")


def _rms(x, g):
    ms = jnp.mean(x * x, axis=-1, keepdims=True)
    return x * lax.rsqrt(ms + EPS) * g


def _rms_pad(x, g, n):
    ms = jnp.sum(x * x, axis=-1, keepdims=True) * (1.0 / n)
    return x * lax.rsqrt(ms + EPS) * g


def _gelu(x):
    c = np.sqrt(2.0 / np.pi).astype(np.float32)
    return x * (0.5 * (1.0 + jnp.tanh(c * (x + 0.044715 * (x * x * x)))))


def _dot(a, b):
    return jnp.dot(a, b, preferred_element_type=F32)


def _dot_nt(a, b):
    return lax.dot_general(a, b, (((1,), (1,)), ((), ())), preferred_element_type=F32)


def _ffn_kernel(x_ref, g_ref, wg_ref, wu_ref, wd_ref, o_ref, *rest, emit_bf16):
    h_ref = rest[-1]
    f = pl.program_id(1)

    @pl.when(f == 0)
    def _():
        h_ref[...] = _rms(x_ref[...], g_ref[...]).astype(BF16)
        o_ref[...] = jnp.zeros_like(o_ref)

    wg, wu, wd = wg_ref[...], wu_ref[...], wd_ref[...]
    if emit_bf16:
        wg, wu, wd = wg.astype(BF16), wu.astype(BF16), wd.astype(BF16)
        rest[0][...], rest[1][...], rest[2][...] = wg, wu, wd
    h = h_ref[...]
    a = _dot(h, wg)
    b = _dot(h, wu)
    act = (a * (1.0 / (1.0 + jnp.exp(-a))) * b).astype(BF16)
    o_ref[...] += _dot(act, wd)

    @pl.when(f == pl.num_programs(1) - 1)
    def _():
        o_ref[...] = x_ref[...] + 0.5 * o_ref[...]


def _ffn(x, g, wg, wu, wd, layer, emit_bf16):
    m = x.shape[0]
    tm = _pick(m, (1024, 512, 256, 128))
    tf = 256
    up_shape, down_shape = (D_MODEL, tf), (tf, D_MODEL)
    if emit_bf16:
        w_specs = [pl.BlockSpec((None,) + up_shape, lambda i, f: (layer, 0, f)),
                   pl.BlockSpec((None,) + up_shape, lambda i, f: (layer, 0, f)),
                   pl.BlockSpec((None,) + down_shape, lambda i, f: (layer, f, 0))]
    else:
        w_specs = [pl.BlockSpec(up_shape, lambda i, f: (0, f)),
                   pl.BlockSpec(up_shape, lambda i, f: (0, f)),
                   pl.BlockSpec(down_shape, lambda i, f: (f, 0))]
    out_specs = [pl.BlockSpec((tm, D_MODEL), lambda i, f: (i, 0))]
    out_shape = [jax.ShapeDtypeStruct((m, D_MODEL), F32)]
    if emit_bf16:
        out_specs += [pl.BlockSpec(up_shape, lambda i, f: (0, f)), pl.BlockSpec(up_shape, lambda i, f: (0, f)),
                      pl.BlockSpec(down_shape, lambda i, f: (f, 0))]
        out_shape += [jax.ShapeDtypeStruct((D_MODEL, D_FF), BF16), jax.ShapeDtypeStruct((D_MODEL, D_FF), BF16),
                      jax.ShapeDtypeStruct((D_FF, D_MODEL), BF16)]
    x_mode = dict(pipeline_mode=pl.Buffered(1)) if emit_bf16 else {}
    res = pl.pallas_call(
        functools.partial(_ffn_kernel, emit_bf16=emit_bf16),
        grid=(m // tm, D_FF // tf),
        in_specs=[pl.BlockSpec((tm, D_MODEL), lambda i, f: (i, 0), **x_mode),
                  pl.BlockSpec((None, 1, D_MODEL), lambda i, f: (layer, 0, 0))] + w_specs,
        out_specs=out_specs,
        out_shape=out_shape,
        scratch_shapes=[pltpu.VMEM((tm, D_MODEL), BF16)],
        compiler_params=_params("arbitrary" if emit_bf16 else "parallel", "arbitrary"),
        name="ffn_cast" if emit_bf16 else "ffn",
    )(x, g, wg, wu, wd)
    return (res[0], tuple(res[1:])) if emit_bf16 else res[0]


def _mixer_kernel(x_ref, c_ref, s1_ref, s2_ref, gmix_ref, win_ref, gqlat_ref, wuq_ref, gqn_ref, gqr_ref,
                  gkv_ref, gkr_ref, gsgu_ref, gmq_ref, *rest, with_kv, sub_rows):
    if with_kv:
        wuk_ref, gkn_ref, wuv_ref, q_ref, lat_ref, kr_ref, u_ref, v_ref, qm_ref, k_ref, vv_ref = rest
    else:
        q_ref, lat_ref, kr_ref, u_ref, v_ref, qm_ref = rest

    for r0 in range(0, x_ref.shape[0], sub_rows):
        rs = slice(r0, r0 + sub_rows)
        h = _rms(x_ref[rs, :], gmix_ref[...]).astype(BF16)
        z = _dot(h, win_ref[...])
        cos, sin_lo, sin_hi = c_ref[rs, :], s1_ref[rs, :], s2_ref[rs, :]

        def rope(t):
            return t * cos + pltpu.roll(t, LANES - 32, 1) * sin_lo + pltpu.roll(t, 32, 1) * sin_hi

        cq = _rms(z[:, _O_CQ:_O_CQ + Q_LORA_RANK], gqlat_ref[...]).astype(BF16)
        q = _dot(cq, wuq_ref[...])
        for hd in range(MLA_HEADS):
            qn = _rms(q[:, hd * LANES:(hd + 1) * LANES], gqn_ref[...])
            qr = rope(_rms_pad(q[:, MLA_WIDTH + hd * LANES:MLA_WIDTH + (hd + 1) * LANES], gqr_ref[...], QK_ROPE_DIM))
            q_ref[rs, hd * QK_PAD:hd * QK_PAD + LANES] = qn.astype(BF16)
            q_ref[rs, hd * QK_PAD + LANES:(hd + 1) * QK_PAD] = qr.astype(BF16)

        lat = _rms(z[:, _O_KV:_O_KV + KV_LORA_RANK], gkv_ref[...])
        lat_ref[rs, :] = lat
        kr = rope(_rms_pad(z[:, _O_KR:_O_KR + LANES], gkr_ref[...], QK_ROPE_DIM))
        kr_ref[rs, :] = kr
        u_ref[rs, :] = _gelu(z[:, _O_U:_O_U + CHUNK_WIDTH])
        v_ref[rs, :] = _rms(_gelu(z[:, _O_V:_O_V + CHUNK_WIDTH]), gsgu_ref[...])
        for hd in range(MEM_HEADS):
            sl = slice(hd * LANES, (hd + 1) * LANES)
            qm_ref[rs, sl] = _rms(z[:, _O_QM + hd * LANES:_O_QM + (hd + 1) * LANES], gmq_ref[...])

        if with_kv:
            latb = lat.astype(BF16)
            kn = _dot(latb, wuk_ref[...])
            krb = kr.astype(BF16)
            for hd in range(MLA_HEADS):
                k_ref[rs, hd * QK_PAD:hd * QK_PAD + LANES] = _rms(kn[:, hd * LANES:(hd + 1) * LANES], gkn_ref[...]).astype(BF16)
                k_ref[rs, hd * QK_PAD + LANES:(hd + 1) * QK_PAD] = krb
            vv_ref[rs, :] = _dot(latb, wuv_ref[...]).astype(BF16)


def _mixer(x, tabs, w, layer, with_kv):
    m = x.shape[0]
    tm = _pick(m, (512, 256, 128))
    sub_rows = min(tm, 256)
    row = lambda n: pl.BlockSpec((tm, n), lambda i: (i, 0))
    gain = lambda n: pl.BlockSpec((None, 1, n), lambda i: (layer, 0, 0))
    wspec = lambda r, c: pl.BlockSpec((None, r, c), lambda i: (layer, 0, 0), pipeline_mode=pl.Buffered(1))
    in_specs = [row(D_MODEL), row(LANES), row(LANES), row(LANES), gain(D_MODEL), wspec(D_MODEL, IN_PAD_WIDTH),
                gain(Q_LORA_RANK), wspec(Q_LORA_RANK, 2 * MLA_WIDTH), gain(LANES), gain(LANES),
                gain(KV_LORA_RANK), gain(LANES), gain(CHUNK_WIDTH), gain(LANES)]
    args = [x, *tabs, w["mix_norm"], w["w_in"], w["q_lat_norm"], w["w_uq"], w["q_nope_norm"], w["q_rope_norm"],
            w["kv_lat_norm"], w["k_rope_norm"], w["sgu_norm"], w["mem_q_norm"]]
    out_specs = [row(MLA_HEADS * QK_PAD), row(KV_LORA_RANK), row(LANES), row(CHUNK_WIDTH), row(CHUNK_WIDTH), row(MEM_WIDTH)]
    out_shape = [jax.ShapeDtypeStruct((m, MLA_HEADS * QK_PAD), BF16), jax.ShapeDtypeStruct((m, KV_LORA_RANK), F32),
                 jax.ShapeDtypeStruct((m, LANES), F32), jax.ShapeDtypeStruct((m, CHUNK_WIDTH), F32),
                 jax.ShapeDtypeStruct((m, CHUNK_WIDTH), F32), jax.ShapeDtypeStruct((m, MEM_WIDTH), F32)]
    if with_kv:
        in_specs += [wspec(KV_LORA_RANK, MLA_WIDTH), gain(LANES), wspec(KV_LORA_RANK, MLA_WIDTH)]
        args += [w["w_uk"], w["k_nope_norm"], w["w_uv"]]
        out_specs += [row(MLA_HEADS * QK_PAD), row(MLA_WIDTH)]
        out_shape += [jax.ShapeDtypeStruct((m, MLA_HEADS * QK_PAD), BF16), jax.ShapeDtypeStruct((m, MLA_WIDTH), BF16)]
    return pl.pallas_call(
        functools.partial(_mixer_kernel, with_kv=with_kv, sub_rows=sub_rows),
        grid=(m // tm,),
        in_specs=in_specs, out_specs=out_specs, out_shape=out_shape,
        compiler_params=_params("parallel"),
        name="mixer_kv" if with_kv else "mixer",
    )(*args)


def _mla_prompt_kernel(q_ref, k_ref, v_ref, o_ref, *, seq, tq):
    rows = lax.broadcasted_iota(jnp.int32, (tq, tq), 0)
    cols = lax.broadcasted_iota(jnp.int32, (tq, tq), 1)
    causal = cols <= rows
    for qi in range(seq // tq):
        q = q_ref[qi * tq:(qi + 1) * tq, :]
        m = jnp.full((tq, 1), -1e30, F32)
        l = jnp.zeros((tq, 1), F32)
        acc = jnp.zeros((tq, V_HEAD_DIM), F32)
        for kb in range(qi + 1):
            s = _dot_nt(q, k_ref[kb * tq:(kb + 1) * tq, :]) * (MLA_SCALE * LOG2E)
            if kb == qi:
                s = jnp.where(causal, s, -jnp.inf)
            m_new = jnp.maximum(m, jnp.max(s, axis=-1, keepdims=True))
            p = jnp.exp2(s - m_new)
            corr = jnp.exp2(m - m_new)
            l = l * corr + jnp.sum(p, axis=-1, keepdims=True)
            acc = acc * corr + _dot(p.astype(BF16), v_ref[kb * tq:(kb + 1) * tq, :])
            m = m_new
        o_ref[qi * tq:(qi + 1) * tq, :] = acc / l


def _mla_prompt(q, k, v, batch, seq):
    tq = _pick(seq, (256, 128))
    return pl.pallas_call(
        functools.partial(_mla_prompt_kernel, seq=seq, tq=tq),
        grid=(batch, MLA_HEADS),
        in_specs=[
            pl.BlockSpec((seq, QK_PAD), lambda b, h: (b, h)),
            pl.BlockSpec((seq, QK_PAD), lambda b, h: (b, h)),
            pl.BlockSpec((seq, V_HEAD_DIM), lambda b, h: (b, h)),
        ],
        out_specs=pl.BlockSpec((seq, V_HEAD_DIM), lambda b, h: (b, h)),
        out_shape=jax.ShapeDtypeStruct((batch * seq, MLA_WIDTH), F32),
        compiler_params=_params("parallel", "parallel"),
        name="mla_prompt",
    )(q, k, v)


def _absorb_kernel(qn_ref, qr_ref, wuk_ref, g_ref, qa_ref, qro_ref, *, bd, t):
    wg = (wuk_ref[...] * g_ref[...]).astype(BF16)
    qa = _dot_nt(qn_ref[...], wg)
    qa_ref[...] = qa.reshape(bd, t, KV_LORA_RANK)
    qro_ref[...] = qr_ref[...].astype(F32).reshape(bd, t, LANES)


def _absorb(q, w, layer, bd, t):
    ms = bd * t
    return pl.pallas_call(
        functools.partial(_absorb_kernel, bd=bd, t=t),
        grid=(MLA_HEADS,),
        in_specs=[
            pl.BlockSpec((ms, LANES), lambda h: (0, 2 * h)),
            pl.BlockSpec((ms, LANES), lambda h: (0, 2 * h + 1)),
            pl.BlockSpec((None, KV_LORA_RANK, LANES), lambda h: (layer, 0, h)),
            pl.BlockSpec((None, 1, LANES), lambda h: (layer, 0, 0)),
        ],
        out_specs=[pl.BlockSpec((bd, t, KV_LORA_RANK), lambda h: (0, h, 0)),
                   pl.BlockSpec((bd, t, LANES), lambda h: (0, h, 0))],
        out_shape=[jax.ShapeDtypeStruct((bd, MLA_HEADS * t, KV_LORA_RANK), F32),
                   jax.ShapeDtypeStruct((bd, MLA_HEADS * t, LANES), F32)],
        compiler_params=_params("parallel"),
        name="mla_absorb",
    )(q, q, w["w_uk_f32"], w["k_nope_norm"])


def _mla_sample_kernel(pt_ref, qa_ref, qr_ref, wukt_ref, latn_ref, krn_ref, clat_hbm, ckr_hbm, o_ref,
                       lat_buf, kr_buf, sem, a_sc, latb_sc, new_lat, new_kr, s_sc, *, layer, n_pages, t, tiles):
    b = pl.program_id(0)
    slot = b % 2
    nq = MLA_HEADS * t
    nk = MLA_HEADS * QK_NOPE_DIM

    def page_copies(bb, sl, j):
        pg = pt_ref[bb * n_pages + j]
        row0 = pl.multiple_of(j * PAGE_SIZE, PAGE_SIZE)
        return (pltpu.make_async_copy(clat_hbm.at[layer, pg], lat_buf.at[sl, pl.ds(row0, PAGE_SIZE)], sem.at[0, sl]),
                pltpu.make_async_copy(ckr_hbm.at[layer, pg], kr_buf.at[sl, j], sem.at[1, sl]))

    @pl.when(b == 0)
    def _():
        a_sc[0:nk, :] = wukt_ref[...]

        def body(j, c):
            for cp in page_copies(0, 0, j):
                cp.start()
            return c
        lax.fori_loop(0, n_pages, body, 0)

    nxt = (b + 1) % pl.num_programs(0)

    def wait_body(j, c):
        for cp in page_copies(b, slot, j):
            cp.wait()
        return c
    lax.fori_loop(0, n_pages, wait_body, 0, unroll=_pick(n_pages, (8, 4, 2, 1)))

    a_sc[nk:nk + nq, :] = qa_ref[0].astype(BF16)
    qr = qr_ref[0][:, 0:QK_ROPE_DIM].astype(BF16)

    def nope_scores(latb):
        st = _dot_nt(a_sc[...], latb)
        out = []
        for hd in range(MLA_HEADS):
            kn = st[hd * QK_NOPE_DIM:(hd + 1) * QK_NOPE_DIM, :]
            rn = lax.rsqrt(jnp.mean(kn * kn, axis=0, keepdims=True) + EPS)
            out.append(st[nk + hd * t:nk + (hd + 1) * t, :] * rn)
        return jnp.concatenate(out, axis=0)

    def lane_tile_max(s):
        out = s[:, 0:LANES]
        for c in range(1, s.shape[1] // LANES):
            out = jnp.maximum(out, s[:, c * LANES:(c + 1) * LANES])
        return out

    def lane_tile_sum(p):
        out = p[:, 0:LANES]
        for c in range(1, p.shape[1] // LANES):
            out = out + p[:, c * LANES:(c + 1) * LANES]
        return out

    def score_tile(row0, rows):
        pages = range(row0 // PAGE_SIZE, (row0 + rows) // PAGE_SIZE)
        for j in pages:
            for cp in page_copies(nxt, 1 - slot, j):
                cp.start()
        latb = lat_buf[slot, row0:row0 + rows, :].astype(BF16)
        latb_sc[row0:row0 + rows, :] = latb
        krt = jnp.concatenate([kr_buf[slot, j].astype(BF16) for j in pages], axis=1)
        s = (nope_scores(latb) + _dot(qr, krt)) * (MLA_SCALE * LOG2E)
        s_sc[:, row0:row0 + rows] = s
        return jnp.max(lane_tile_max(s), axis=-1, keepdims=True)

    def merge(state, smax, s, latb):
        m, acc, psum = state
        m_new = jnp.maximum(m, smax)
        corr = jnp.exp2(m - m_new)
        p = jnp.exp2(s - m_new)
        return m_new, acc * corr + _dot(p.astype(BF16), latb), psum * corr + lane_tile_sum(p)

    def merge_tile(state, smax, row0, rows):
        return merge(state, smax, s_sc[:, row0:row0 + rows], latb_sc[row0:row0 + rows, :])

    new_lat[...] = jnp.zeros_like(new_lat)
    new_kr[...] = jnp.zeros_like(new_kr)
    new_lat[0:t, :] = latn_ref[...]
    new_kr[0:t, :] = krn_ref[:, 0:QK_ROPE_DIM]
    newl = new_lat[...].astype(BF16)
    s_new = (nope_scores(newl) + _dot_nt(qr, new_kr[...].astype(BF16))) * (MLA_SCALE * LOG2E)
    rows = lax.broadcasted_iota(jnp.int32, (nq, PAGE_SIZE), 0)
    cols = lax.broadcasted_iota(jnp.int32, (nq, PAGE_SIZE), 1)
    s_new = jnp.where(cols <= rows % t, s_new, -jnp.inf)
    state = (jnp.full((nq, 1), -1e30, F32), jnp.zeros((nq, KV_LORA_RANK), F32), jnp.zeros((nq, LANES), F32))
    state = merge(state, jnp.max(s_new, axis=-1, keepdims=True), s_new, newl)

    smax_prev = score_tile(*tiles[0])
    for i in range(1, len(tiles)):
        smax = score_tile(*tiles[i])
        state = merge_tile(state, smax_prev, *tiles[i - 1])
        smax_prev = smax
    _, acc, psum = merge_tile(state, smax_prev, *tiles[-1])
    o_ref[0] = acc / jnp.sum(psum, axis=-1, keepdims=True)

    @pl.when(b == pl.num_programs(0) - 1)
    def _():
        def body(j, c):
            for cp in page_copies(nxt, 1 - slot, j):
                cp.wait()
            return c
        lax.fori_loop(0, n_pages, body, 0)


def _sample_tiles(tk):
    ts = _pick(tk, (2048, 1024, 512, 256, 128))
    return tuple((r0, ts) for r0 in range(0, tk, ts))


def _mla_sample(pt_flat, qa, qr, lat_new, kr_new, cache_lat, cache_kr_t, w, layer, bd, t, n_pages):
    nq = MLA_HEADS * t
    tk = n_pages * PAGE_SIZE
    grid_spec = pltpu.PrefetchScalarGridSpec(
        num_scalar_prefetch=1,
        grid=(bd,),
        in_specs=[
            pl.BlockSpec((1, nq, KV_LORA_RANK), lambda b, pt: (b, 0, 0)),
            pl.BlockSpec((1, nq, LANES), lambda b, pt: (b, 0, 0)),
            pl.BlockSpec((None, MLA_HEADS * QK_NOPE_DIM, KV_LORA_RANK), lambda b, pt: (layer, 0, 0)),
            pl.BlockSpec((t, KV_LORA_RANK), lambda b, pt: (b, 0)),
            pl.BlockSpec((t, LANES), lambda b, pt: (b, 0)),
            pl.BlockSpec(memory_space=pl.ANY),
            pl.BlockSpec(memory_space=pl.ANY),
        ],
        out_specs=pl.BlockSpec((1, nq, KV_LORA_RANK), lambda b, pt: (b, 0, 0)),
        scratch_shapes=[
            pltpu.VMEM((2, tk, KV_LORA_RANK), F32),
            pltpu.VMEM((2, n_pages, QK_ROPE_DIM, PAGE_SIZE), F32),
            pltpu.SemaphoreType.DMA((2, 2)),
            pltpu.VMEM((MLA_HEADS * QK_NOPE_DIM + nq, KV_LORA_RANK), BF16),
            pltpu.VMEM((tk, KV_LORA_RANK), BF16),
            pltpu.VMEM((PAGE_SIZE, KV_LORA_RANK), F32),
            pltpu.VMEM((PAGE_SIZE, QK_ROPE_DIM), F32),
            pltpu.VMEM((nq, tk), F32),
        ],
    )
    return pl.pallas_call(
        functools.partial(_mla_sample_kernel, layer=layer, n_pages=n_pages, t=t, tiles=_sample_tiles(tk)),
        grid_spec=grid_spec,
        out_shape=jax.ShapeDtypeStruct((bd, nq, KV_LORA_RANK), F32),
        compiler_params=_params("arbitrary"),
        name="mla_sample",
    )(pt_flat, qa, qr, w["w_uk_t"], lat_new, kr_new, cache_lat, cache_kr_t)


def _uv_up_kernel(o_ref, wuv_ref, out_ref, *, bd, t):
    o = o_ref[...].reshape(bd * t, KV_LORA_RANK).astype(BF16)
    out_ref[...] = _dot(o, wuv_ref[...])


def _uv_up(o_lat, w, layer, bd, t):
    return pl.pallas_call(
        functools.partial(_uv_up_kernel, bd=bd, t=t),
        grid=(MLA_HEADS,),
        in_specs=[pl.BlockSpec((bd, t, KV_LORA_RANK), lambda h: (0, h, 0)),
                  pl.BlockSpec((None, KV_LORA_RANK, V_HEAD_DIM), lambda h: (layer, 0, h))],
        out_specs=pl.BlockSpec((bd * t, V_HEAD_DIM), lambda h: (0, h)),
        out_shape=jax.ShapeDtypeStruct((bd * t, MLA_WIDTH), F32),
        compiler_params=_params("parallel"),
        name="mla_uv_up",
    )(o_lat, w["w_uv"])


def _chunk_kernel(u_ref, v_ref, w_ref, b_ref, o_ref):
    rows = lax.broadcasted_iota(jnp.int32, (CHUNK, CHUNK), 0)
    cols = lax.broadcasted_iota(jnp.int32, (CHUNK, CHUNK), 1)
    causal = cols <= rows
    for hd in range(CHUNK_HEADS):
        sl = slice(hd * LANES, (hd + 1) * LANES)
        wm = jnp.where(causal, w_ref[hd], 0.0).astype(BF16)
        bias = b_ref[:, hd:hd + 1]
        for r0 in range(0, u_ref.shape[0], CHUNK):
            rs = slice(r0, r0 + CHUNK)
            s = _dot(wm, v_ref[rs, sl].astype(BF16)) + bias
            o_ref[rs, sl] = u_ref[rs, sl] * s


def _chunk_mlp(u, v, w_sp, b_sp, layer):
    m = u.shape[0]
    tm = _pick(m, (512, 256, 128))
    return pl.pallas_call(
        _chunk_kernel,
        grid=(m // tm,),
        in_specs=[pl.BlockSpec((tm, CHUNK_WIDTH), lambda c: (c, 0)),
                  pl.BlockSpec((tm, CHUNK_WIDTH), lambda c: (c, 0)),
                  pl.BlockSpec((None, CHUNK_HEADS, CHUNK, CHUNK), lambda c: (layer, 0, 0, 0)),
                  pl.BlockSpec((None, CHUNK, CHUNK_HEADS), lambda c: (layer, 0, 0))],
        out_specs=pl.BlockSpec((tm, CHUNK_WIDTH), lambda c: (c, 0)),
        out_shape=jax.ShapeDtypeStruct((m, CHUNK_WIDTH), F32),
        compiler_params=_params("parallel"),
        name="chunk_mlp",
    )(u, v, w_sp, b_sp)


def _mem_kv_kernel(x_ref, g_ref, w_ref, gk_ref, k_ref, v_ref):
    h = _rms(x_ref[...], g_ref[...]).astype(BF16)
    kv = _dot(h, w_ref[...])
    for hd in range(MEM_HEADS):
        sl = slice(hd * LANES, (hd + 1) * LANES)
        k_ref[:, sl] = _rms(kv[:, sl], gk_ref[...])
    v_ref[...] = kv[:, MEM_WIDTH:]


def _mem_kv(mem, w, layer):
    m = mem.shape[0]
    tm = _pick(m, (256, 128))
    return pl.pallas_call(
        _mem_kv_kernel,
        grid=(m // tm,),
        in_specs=[pl.BlockSpec((tm, D_MODEL), lambda i: (i, 0)),
                  pl.BlockSpec((None, 1, D_MODEL), lambda i: (layer, 0, 0)),
                  pl.BlockSpec((None, D_MODEL, 2 * MEM_WIDTH), lambda i: (layer, 0, 0)),
                  pl.BlockSpec((None, 1, LANES), lambda i: (layer, 0, 0))],
        out_specs=[pl.BlockSpec((tm, MEM_WIDTH), lambda i: (i, 0))] * 2,
        out_shape=[jax.ShapeDtypeStruct((m, MEM_WIDTH), F32)] * 2,
        compiler_params=_params("parallel"),
        name="mem_kv",
    )(mem, w["mem_norm"], w["w_mem_kv"], w["mem_k_norm"])


def _softmax_rows(s):
    e = jnp.exp(s - jnp.max(s, axis=-1, keepdims=True))
    return e / jnp.sum(e, axis=-1, keepdims=True)


def _mem_attn_prompt_kernel(q_ref, k_ref, v_ref, o_ref):
    for hd in range(MEM_HEADS):
        sl = slice(hd * LANES, (hd + 1) * LANES)
        s = _dot_nt(q_ref[:, sl].astype(BF16), k_ref[:, sl].astype(BF16)) * MEM_SCALE
        o_ref[:, sl] = _dot(_softmax_rows(s).astype(BF16), v_ref[:, sl].astype(BF16))


def _mem_attn_prompt(qm, mk, mv, batch, seq):
    tq = _pick(seq, (1024, 512, 256, 128))
    nq = seq // tq
    return pl.pallas_call(
        _mem_attn_prompt_kernel,
        grid=(batch, nq),
        in_specs=[pl.BlockSpec((tq, MEM_WIDTH), lambda b, i: (b * nq + i, 0)),
                  pl.BlockSpec((N_MEM, MEM_WIDTH), lambda b, i: (b, 0)),
                  pl.BlockSpec((N_MEM, MEM_WIDTH), lambda b, i: (b, 0))],
        out_specs=pl.BlockSpec((tq, MEM_WIDTH), lambda b, i: (b * nq + i, 0)),
        out_shape=jax.ShapeDtypeStruct((batch * seq, MEM_WIDTH), F32),
        compiler_params=_params("parallel", "parallel"),
        name="mem_attn_prompt",
    )(qm, mk, mv)


def _mem_attn_sample_kernel(q_ref, k_ref, v_ref, o_ref, *, bb, t):
    for hd in range(MEM_HEADS):
        sl = slice(hd * LANES, (hd + 1) * LANES)
        q3 = q_ref[:, sl].reshape(bb, t, MEM_HEAD_DIM).astype(BF16)
        k3 = k_ref[:, pl.ds(hd, N_MEM, stride=MEM_HEADS), :].astype(BF16)
        v3 = v_ref[:, pl.ds(hd, N_MEM, stride=MEM_HEADS), :].astype(BF16)
        s = jnp.einsum("bqd,bkd->bqk", q3, k3, preferred_element_type=F32) * MEM_SCALE
        o3 = jnp.einsum("bqk,bkd->bqd", _softmax_rows(s).astype(BF16), v3, preferred_element_type=F32)
        o_ref[:, sl] = o3.reshape(bb * t, MEM_HEAD_DIM)


def _mem_attn_sample(qm, cache_k, cache_v, layer, bd, t):
    bb = _pick(bd, (16, 8, 4, 2, 1))
    kv_spec = pl.BlockSpec((None, bb, N_MEM * MEM_HEADS, MEM_HEAD_DIM), lambda i: (layer, i, 0, 0))
    return pl.pallas_call(
        functools.partial(_mem_attn_sample_kernel, bb=bb, t=t),
        grid=(bd // bb,),
        in_specs=[pl.BlockSpec((bb * t, MEM_WIDTH), lambda i: (i, 0)), kv_spec, kv_spec],
        out_specs=pl.BlockSpec((bb * t, MEM_WIDTH), lambda i: (i, 0)),
        out_shape=jax.ShapeDtypeStruct((bd * t, MEM_WIDTH), F32),
        compiler_params=_params("parallel"),
        name="mem_attn_sample",
    )(qm, cache_k, cache_v)


def _merge_kernel(x_ref, a_ref, c_ref, m_ref, g_ref, w_ref, o_ref):
    g = g_ref[...]
    c0, c1 = MLA_WIDTH, MLA_WIDTH + CHUNK_WIDTH
    acc = _dot(_rms(a_ref[...], g[:, 0:c0]).astype(BF16), w_ref[0:c0, :])
    acc += _dot(_rms(c_ref[...], g[:, c0:c1]).astype(BF16), w_ref[c0:c1, :])
    acc += _dot(_rms(m_ref[...], g[:, c1:]).astype(BF16), w_ref[c1:, :])
    o_ref[...] = x_ref[...] + acc


def _merge(x, o_mla, o_chunk, o_mem, w, layer):
    m = x.shape[0]
    tm = _pick(m, (512, 256, 128))
    row = lambda n: pl.BlockSpec((tm, n), lambda i: (i, 0))
    return pl.pallas_call(
        _merge_kernel,
        grid=(m // tm,),
        in_specs=[row(D_MODEL), row(MLA_WIDTH), row(CHUNK_WIDTH), row(MEM_WIDTH),
                  pl.BlockSpec((None, 1, D_MODEL), lambda i: (layer, 0, 0)),
                  pl.BlockSpec((None, D_MODEL, D_MODEL), lambda i: (layer, 0, 0))],
        out_specs=row(D_MODEL),
        out_shape=jax.ShapeDtypeStruct((m, D_MODEL), F32),
        compiler_params=_params("parallel"),
        name="merge_heads",
    )(x, o_mla, o_chunk, o_mem, w["out_norm"], w["w_o"])


def _rope_tables(pos):
    inv = ROPE_THETA ** (-jnp.arange(0, QK_ROPE_DIM, 2, dtype=F32) / QK_ROPE_DIM)
    ang = pos.astype(F32)[:, None] * inv[None, :]
    c, s = jnp.cos(ang), jnp.sin(ang)
    z32, z64 = jnp.zeros_like(c), jnp.zeros((pos.shape[0], LANES - QK_ROPE_DIM), F32)
    return (jnp.concatenate([c, c, z64], axis=-1),
            jnp.concatenate([-s, z32, z64], axis=-1),
            jnp.concatenate([z32, s, z64], axis=-1))


def _pad_lanes(a, width):
    return jnp.pad(a, [(0, 0)] * (a.ndim - 1) + [(0, width - a.shape[-1])])


def _prepare_weights(w_in, w_uq, w_uk, w_uv, w_spatial, b_spatial, bd, t, gains):
    depth = w_in.shape[0]
    b = np.cumsum([Q_LORA_RANK, KV_LORA_RANK, QK_ROPE_DIM, CHUNK_WIDTH, CHUNK_WIDTH]).tolist()
    c_q, c_kv, k_r, u, v, q_m = (w_in[..., :b[0]], w_in[..., b[0]:b[1]], w_in[..., b[1]:b[2]],
                                 w_in[..., b[2]:b[3]], w_in[..., b[3]:b[4]], w_in[..., b[4]:])
    w = {}
    w["w_in"] = jnp.concatenate([c_q, c_kv, u, v, q_m, _pad_lanes(k_r, LANES)], axis=-1).astype(BF16)
    uq = w_uq.reshape(depth, Q_LORA_RANK, MLA_HEADS, QK_NOPE_DIM + QK_ROPE_DIM)
    w["w_uq"] = jnp.concatenate(
        [uq[..., :QK_NOPE_DIM].reshape(depth, Q_LORA_RANK, MLA_WIDTH),
         _pad_lanes(uq[..., QK_NOPE_DIM:], LANES).reshape(depth, Q_LORA_RANK, MLA_WIDTH)], axis=-1).astype(BF16)
    uk = w_uk.reshape(depth, KV_LORA_RANK, MLA_HEADS * QK_NOPE_DIM)
    w["w_uk_f32"] = uk
    w["w_uk"] = uk.astype(BF16)
    w["w_uk_t"] = jnp.swapaxes(uk, 1, 2).astype(BF16)
    w["w_uv"] = w_uv.reshape(depth, KV_LORA_RANK, MLA_WIDTH).astype(BF16)
    reps = CHUNK // t
    eye = jnp.eye(reps, dtype=F32)
    w_blk = jnp.einsum("ab,lhts->lhatbs", eye, w_spatial[:, :, :t, :t]).reshape(depth, CHUNK_HEADS, CHUNK, CHUNK)
    w["w_sp_prompt"] = w_spatial
    w["w_sp_sample"] = w_blk
    w["b_sp_prompt"] = jnp.swapaxes(b_spatial, 1, 2)
    w["b_sp_sample"] = jnp.tile(jnp.swapaxes(b_spatial[:, :, :t], 1, 2), (1, reps, 1))
    for name, gval in gains.items():
        gval = gval[:, None, :]
        if gval.shape[-1] < LANES:
            gval = _pad_lanes(gval, LANES)
        w[name] = gval
    return w


def kernel(x_prompt, x_sample, mem_prompt, cache_kv_latent, cache_k_rope, cache_mem_k, cache_mem_v, page_table, ffn1_norm, ffn1_w_gate, ffn1_w_up, ffn1_w_down, mix_norm, w_in, q_lat_norm, w_uq, q_nope_norm, q_rope_norm, kv_lat_norm, w_uk, w_uv, k_nope_norm, k_rope_norm, sgu_norm, w_spatial, b_spatial, mem_norm, w_mem_kv, mem_q_norm, mem_k_norm, out_norm, w_o, ffn2_norm, ffn2_w_gate, ffn2_w_up, ffn2_w_down):
    batch, seq, _ = x_prompt.shape
    bd, t, _ = x_sample.shape
    depth = w_in.shape[0]
    n_pages = page_table.shape[1]
    past_len = n_pages * PAGE_SIZE
    assert (bd * t) % CHUNK == 0 and CHUNK % t == 0 and seq % CHUNK == 0

    gains = dict(mix_norm=mix_norm, q_lat_norm=q_lat_norm, q_nope_norm=q_nope_norm, q_rope_norm=q_rope_norm,
                 kv_lat_norm=kv_lat_norm, k_nope_norm=k_nope_norm, k_rope_norm=k_rope_norm, sgu_norm=sgu_norm,
                 mem_norm=mem_norm, mem_q_norm=mem_q_norm, mem_k_norm=mem_k_norm, out_norm=out_norm,
                 ffn1_norm=ffn1_norm, ffn2_norm=ffn2_norm)
    w = _prepare_weights(w_in, w_uq, w_uk, w_uv, w_spatial, b_spatial, bd, t, gains)
    w["w_mem_kv"] = w_mem_kv.astype(BF16)
    w["w_o"] = w_o.astype(BF16)
    ffn_w = [(w["ffn1_norm"], ffn1_w_gate, ffn1_w_up, ffn1_w_down),
             (w["ffn2_norm"], ffn2_w_gate, ffn2_w_up, ffn2_w_down)]

    tabs_p = _rope_tables(jnp.tile(jnp.arange(seq, dtype=jnp.int32), batch))
    tabs_s = _rope_tables(jnp.tile(past_len + jnp.arange(t, dtype=jnp.int32), bd))
    pt_flat = page_table.reshape(-1)
    cache_mk = cache_mem_k.reshape(depth, bd, N_MEM * MEM_HEADS, MEM_HEAD_DIM)
    cache_mv = cache_mem_v.reshape(depth, bd, N_MEM * MEM_HEADS, MEM_HEAD_DIM)
    cache_kr_t = jnp.swapaxes(cache_k_rope, 2, 3)

    xp = x_prompt.reshape(batch * seq, D_MODEL)
    xs = x_sample.reshape(bd * t, D_MODEL)
    mem = mem_prompt.reshape(batch * N_MEM, D_MODEL)
    outs = [[] for _ in range(7)]
    for l in range(depth):
        xs, w1_bf16 = _ffn(xs, *ffn_w[0], l, True)
        q, lat, kr, u, v, qm = _mixer(xs, tabs_s, w, l, False)
        qa, qr = _absorb(q, w, l, bd, t)
        o_lat = _mla_sample(pt_flat, qa, qr, lat, kr, cache_kv_latent, cache_kr_t, w, l, bd, t, n_pages)
        o_mla = _uv_up(o_lat, w, l, bd, t)
        o_chunk = _chunk_mlp(u, v, w["w_sp_sample"], w["b_sp_sample"], l)
        o_mem = _mem_attn_sample(qm, cache_mk, cache_mv, l, bd, t)
        xs = _merge(xs, o_mla, o_chunk, o_mem, w, l)
        outs[4].append(lat.reshape(bd, t, KV_LORA_RANK))
        outs[5].append(kr[:, :QK_ROPE_DIM].reshape(bd, t, QK_ROPE_DIM))
        outs[6].append(v.reshape(bd, t, CHUNK_WIDTH))
        xs, w2_bf16 = _ffn(xs, *ffn_w[1], l, True)
        xp = _ffn(xp, ffn_w[0][0], *w1_bf16, l, False)
        q, lat, kr, u, v, qm, k, vv = _mixer(xp, tabs_p, w, l, True)
        mk, mv = _mem_kv(mem, w, l)
        o_mla = _mla_prompt(q, k, vv, batch, seq)
        o_chunk = _chunk_mlp(u, v, w["w_sp_prompt"], w["b_sp_prompt"], l)
        o_mem = _mem_attn_prompt(qm, mk, mv, batch, seq)
        xp = _merge(xp, o_mla, o_chunk, o_mem, w, l)
        outs[0].append(lat.reshape(batch, seq, KV_LORA_RANK))
        outs[1].append(kr[:, :QK_ROPE_DIM].reshape(batch, seq, QK_ROPE_DIM))
        outs[2].append(mk.reshape(batch, N_MEM, MEM_HEADS, MEM_HEAD_DIM))
        outs[3].append(mv.reshape(batch, N_MEM, MEM_HEADS, MEM_HEAD_DIM))
        xp = _ffn(xp, ffn_w[1][0], *w2_bf16, l, False)
    return (xp.reshape(batch, seq, D_MODEL), xs.reshape(bd, t, D_MODEL), *[jnp.stack(o) for o in outs])
```

```python
import functools

import numpy as np
import jax
import jax.numpy as jnp
from jax import lax
from jax.experimental import pallas as pl
from jax.experimental.pallas import tpu as pltpu

F32 = jnp.float32
BF16 = jnp.bfloat16

D_MODEL = 2048
D_FF = 5632
MLA_HEADS = 8
QK_NOPE_DIM = 128
QK_ROPE_DIM = 64
V_HEAD_DIM = 128
Q_LORA_RANK = 512
KV_LORA_RANK = 256
MLA_WIDTH = MLA_HEADS * V_HEAD_DIM
MLA_SCALE = (QK_NOPE_DIM + QK_ROPE_DIM) ** -0.5
ROPE_THETA = 10000.0
PAGE_SIZE = 128
CHUNK = 128
CHUNK_HEADS = 4
CHUNK_WIDTH = 512
N_MEM = 256
MEM_HEADS = 4
MEM_HEAD_DIM = 128
MEM_WIDTH = 512
MEM_SCALE = MEM_HEAD_DIM ** -0.5
EPS = 1e-6
LOG2E = 1.4426950408889634

LANES = 128
QK_PAD = 2 * LANES
IN_PAD_WIDTH = Q_LORA_RANK + KV_LORA_RANK + 2 * CHUNK_WIDTH + MEM_WIDTH + LANES
VMEM_LIMIT = 56 * 1024 * 1024
FFN_VMEM_LIMIT = 60 * 1024 * 1024

_O_CQ, _O_KV, _O_U, _O_V, _O_QM, _O_KR = 0, 512, 768, 1280, 1792, 2304


def _params(*sem):
    return pltpu.CompilerParams(dimension_semantics=sem, vmem_limit_bytes=VMEM_LIMIT)


def _pick(n, cands):
    for c in cands:
        if n % c == 0:
            return c
    raise ValueError(f"no tile for {n}")


def _rms(x, g):
    ms = jnp.mean(x * x, axis=-1, keepdims=True)
    return x * lax.rsqrt(ms + EPS) * g


def _rms_pad(x, g, n):
    ms = jnp.sum(x * x, axis=-1, keepdims=True) * (1.0 / n)
    return x * lax.rsqrt(ms + EPS) * g


def _gelu(x):
    c = np.sqrt(2.0 / np.pi).astype(np.float32)
    return x * (0.5 * (1.0 + jnp.tanh(c * (x + 0.044715 * (x * x * x)))))


def _dot(a, b):
    return jnp.dot(a, b, preferred_element_type=F32)


def _dot_nt(a, b):
    return lax.dot_general(a, b, (((1,), (1,)), ((), ())), preferred_element_type=F32)


def _ffn_kernel(x_ref, g_ref, wg_ref, wu_ref, wd_ref, o_ref, *rest, emit_bf16):
    h_ref = rest[-1]
    f = pl.program_id(1)

    @pl.when(f == 0)
    def _():
        h_ref[...] = _rms(x_ref[...], g_ref[...]).astype(BF16)
        o_ref[...] = jnp.zeros_like(o_ref)

    wg, wu, wd = wg_ref[...], wu_ref[...], wd_ref[...]
    if emit_bf16:
        wg, wu, wd = wg.astype(BF16), wu.astype(BF16), wd.astype(BF16)
        rest[0][...], rest[1][...], rest[2][...] = wg, wu, wd
    h = h_ref[...]
    a = _dot(h, wg)
    b = _dot(h, wu)
    act = (a * (1.0 / (1.0 + jnp.exp(-a))) * b).astype(BF16)
    o_ref[...] += _dot(act, wd)

    @pl.when(f == pl.num_programs(1) - 1)
    def _():
        o_ref[...] = x_ref[...] + 0.5 * o_ref[...]


def _ffn(x, g, wg, wu, wd, layer, emit_bf16):
    m = x.shape[0]
    tm = _pick(m, (1024, 512, 256, 128))
    tf = 256 if emit_bf16 else 512
    up_shape, down_shape = (D_MODEL, tf), (tf, D_MODEL)
    if emit_bf16:
        w_specs = [pl.BlockSpec((None,) + up_shape, lambda i, f: (layer, 0, f)),
                   pl.BlockSpec((None,) + up_shape, lambda i, f: (layer, 0, f)),
                   pl.BlockSpec((None,) + down_shape, lambda i, f: (layer, f, 0))]
    else:
        w_specs = [pl.BlockSpec(up_shape, lambda i, f: (0, f)),
                   pl.BlockSpec(up_shape, lambda i, f: (0, f)),
                   pl.BlockSpec(down_shape, lambda i, f: (f, 0))]
    out_specs = [pl.BlockSpec((tm, D_MODEL), lambda i, f: (i, 0))]
    out_shape = [jax.ShapeDtypeStruct((m, D_MODEL), F32)]
    if emit_bf16:
        out_specs += [pl.BlockSpec(up_shape, lambda i, f: (0, f)), pl.BlockSpec(up_shape, lambda i, f: (0, f)),
                      pl.BlockSpec(down_shape, lambda i, f: (f, 0))]
        out_shape += [jax.ShapeDtypeStruct((D_MODEL, D_FF), BF16), jax.ShapeDtypeStruct((D_MODEL, D_FF), BF16),
                      jax.ShapeDtypeStruct((D_FF, D_MODEL), BF16)]
    x_mode = dict(pipeline_mode=pl.Buffered(1)) if emit_bf16 else {}
    res = pl.pallas_call(
        functools.partial(_ffn_kernel, emit_bf16=emit_bf16),
        grid=(m // tm, D_FF // tf),
        in_specs=[pl.BlockSpec((tm, D_MODEL), lambda i, f: (i, 0), **x_mode),
                  pl.BlockSpec((None, 1, D_MODEL), lambda i, f: (layer, 0, 0))] + w_specs,
        out_specs=out_specs,
        out_shape=out_shape,
        scratch_shapes=[pltpu.VMEM((tm, D_MODEL), BF16)],
        compiler_params=pltpu.CompilerParams(
            dimension_semantics=("arbitrary" if emit_bf16 else "parallel", "arbitrary"),
            vmem_limit_bytes=FFN_VMEM_LIMIT),
        name="ffn_cast" if emit_bf16 else "ffn",
    )(x, g, wg, wu, wd)
    return (res[0], tuple(res[1:])) if emit_bf16 else res[0]


def _mixer_kernel(x_ref, c_ref, s1_ref, s2_ref, gmix_ref, win_ref, gqlat_ref, wuq_ref, gqn_ref, gqr_ref,
                  gkv_ref, gkr_ref, gsgu_ref, gmq_ref, *rest, with_kv, sub_rows):
    if with_kv:
        wuk_ref, gkn_ref, wuv_ref, q_ref, lat_ref, kr_ref, u_ref, v_ref, qm_ref, k_ref, vv_ref = rest
    else:
        q_ref, lat_ref, kr_ref, u_ref, v_ref, qm_ref = rest

    for r0 in range(0, x_ref.shape[0], sub_rows):
        rs = slice(r0, r0 + sub_rows)
        h = _rms(x_ref[rs, :], gmix_ref[...]).astype(BF16)
        z = _dot(h, win_ref[...])
        cos, sin_lo, sin_hi = c_ref[rs, :], s1_ref[rs, :], s2_ref[rs, :]

        def rope(t):
            return t * cos + pltpu.roll(t, LANES - 32, 1) * sin_lo + pltpu.roll(t, 32, 1) * sin_hi

        cq = _rms(z[:, _O_CQ:_O_CQ + Q_LORA_RANK], gqlat_ref[...]).astype(BF16)
        q = _dot(cq, wuq_ref[...])
        for hd in range(MLA_HEADS):
            qn = _rms(q[:, hd * LANES:(hd + 1) * LANES], gqn_ref[...])
            qr = rope(_rms_pad(q[:, MLA_WIDTH + hd * LANES:MLA_WIDTH + (hd + 1) * LANES], gqr_ref[...], QK_ROPE_DIM))
            q_ref[rs, hd * QK_PAD:hd * QK_PAD + LANES] = qn.astype(BF16)
            q_ref[rs, hd * QK_PAD + LANES:(hd + 1) * QK_PAD] = qr.astype(BF16)

        lat = _rms(z[:, _O_KV:_O_KV + KV_LORA_RANK], gkv_ref[...])
        lat_ref[rs, :] = lat
        kr = rope(_rms_pad(z[:, _O_KR:_O_KR + LANES], gkr_ref[...], QK_ROPE_DIM))
        kr_ref[rs, :] = kr
        u_ref[rs, :] = _gelu(z[:, _O_U:_O_U + CHUNK_WIDTH])
        v_ref[rs, :] = _rms(_gelu(z[:, _O_V:_O_V + CHUNK_WIDTH]), gsgu_ref[...])
        for hd in range(MEM_HEADS):
            sl = slice(hd * LANES, (hd + 1) * LANES)
            qm_ref[rs, sl] = _rms(z[:, _O_QM + hd * LANES:_O_QM + (hd + 1) * LANES], gmq_ref[...])

        if with_kv:
            latb = lat.astype(BF16)
            kn = _dot(latb, wuk_ref[...])
            krb = kr.astype(BF16)
            for hd in range(MLA_HEADS):
                k_ref[rs, hd * QK_PAD:hd * QK_PAD + LANES] = _rms(kn[:, hd * LANES:(hd + 1) * LANES], gkn_ref[...]).astype(BF16)
                k_ref[rs, hd * QK_PAD + LANES:(hd + 1) * QK_PAD] = krb
            vv_ref[rs, :] = _dot(latb, wuv_ref[...]).astype(BF16)


def _mixer(x, tabs, w, layer, with_kv):
    m = x.shape[0]
    tm = _pick(m, (512, 256, 128))
    sub_rows = min(tm, 256)
    row = lambda n: pl.BlockSpec((tm, n), lambda i: (i, 0))
    gain = lambda n: pl.BlockSpec((None, 1, n), lambda i: (layer, 0, 0))
    wspec = lambda r, c: pl.BlockSpec((None, r, c), lambda i: (layer, 0, 0), pipeline_mode=pl.Buffered(1))
    in_specs = [row(D_MODEL), row(LANES), row(LANES), row(LANES), gain(D_MODEL), wspec(D_MODEL, IN_PAD_WIDTH),
                gain(Q_LORA_RANK), wspec(Q_LORA_RANK, 2 * MLA_WIDTH), gain(LANES), gain(LANES),
                gain(KV_LORA_RANK), gain(LANES), gain(CHUNK_WIDTH), gain(LANES)]
    args = [x, *tabs, w["mix_norm"], w["w_in"], w["q_lat_norm"], w["w_uq"], w["q_nope_norm"], w["q_rope_norm"],
            w["kv_lat_norm"], w["k_rope_norm"], w["sgu_norm"], w["mem_q_norm"]]
    out_specs = [row(MLA_HEADS * QK_PAD), row(KV_LORA_RANK), row(LANES), row(CHUNK_WIDTH), row(CHUNK_WIDTH), row(MEM_WIDTH)]
    out_shape = [jax.ShapeDtypeStruct((m, MLA_HEADS * QK_PAD), BF16), jax.ShapeDtypeStruct((m, KV_LORA_RANK), F32),
                 jax.ShapeDtypeStruct((m, LANES), F32), jax.ShapeDtypeStruct((m, CHUNK_WIDTH), F32),
                 jax.ShapeDtypeStruct((m, CHUNK_WIDTH), F32), jax.ShapeDtypeStruct((m, MEM_WIDTH), F32)]
    if with_kv:
        in_specs += [wspec(KV_LORA_RANK, MLA_WIDTH), gain(LANES), wspec(KV_LORA_RANK, MLA_WIDTH)]
        args += [w["w_uk"], w["k_nope_norm"], w["w_uv"]]
        out_specs += [row(MLA_HEADS * QK_PAD), row(MLA_WIDTH)]
        out_shape += [jax.ShapeDtypeStruct((m, MLA_HEADS * QK_PAD), BF16), jax.ShapeDtypeStruct((m, MLA_WIDTH), BF16)]
    return pl.pallas_call(
        functools.partial(_mixer_kernel, with_kv=with_kv, sub_rows=sub_rows),
        grid=(m // tm,),
        in_specs=in_specs, out_specs=out_specs, out_shape=out_shape,
        compiler_params=_params("parallel"),
        name="mixer_kv" if with_kv else "mixer",
    )(*args)


def _mla_prompt_kernel(q_ref, k_ref, v_ref, o_ref, *, seq, tq):
    rows = lax.broadcasted_iota(jnp.int32, (tq, tq), 0)
    cols = lax.broadcasted_iota(jnp.int32, (tq, tq), 1)
    causal = cols <= rows
    for qi in range(seq // tq):
        q = q_ref[qi * tq:(qi + 1) * tq, :]
        m = jnp.full((tq, 1), -1e30, F32)
        l = jnp.zeros((tq, 1), F32)
        acc = jnp.zeros((tq, V_HEAD_DIM), F32)
        for kb in range(qi + 1):
            s = _dot_nt(q, k_ref[kb * tq:(kb + 1) * tq, :]) * (MLA_SCALE * LOG2E)
            if kb == qi:
                s = jnp.where(causal, s, -jnp.inf)
            m_new = jnp.maximum(m, jnp.max(s, axis=-1, keepdims=True))
            p = jnp.exp2(s - m_new)
            corr = jnp.exp2(m - m_new)
            l = l * corr + jnp.sum(p, axis=-1, keepdims=True)
            acc = acc * corr + _dot(p.astype(BF16), v_ref[kb * tq:(kb + 1) * tq, :])
            m = m_new
        o_ref[qi * tq:(qi + 1) * tq, :] = acc / l


def _mla_prompt(q, k, v, batch, seq):
    tq = _pick(seq, (256, 128))
    return pl.pallas_call(
        functools.partial(_mla_prompt_kernel, seq=seq, tq=tq),
        grid=(batch, MLA_HEADS),
        in_specs=[
            pl.BlockSpec((seq, QK_PAD), lambda b, h: (b, h)),
            pl.BlockSpec((seq, QK_PAD), lambda b, h: (b, h)),
            pl.BlockSpec((seq, V_HEAD_DIM), lambda b, h: (b, h)),
        ],
        out_specs=pl.BlockSpec((seq, V_HEAD_DIM), lambda b, h: (b, h)),
        out_shape=jax.ShapeDtypeStruct((batch * seq, MLA_WIDTH), F32),
        compiler_params=_params("parallel", "parallel"),
        name="mla_prompt",
    )(q, k, v)


def _absorb_kernel(qn_ref, qr_ref, wuk_ref, g_ref, qa_ref, qro_ref, *, bd, t):
    wg = (wuk_ref[...] * g_ref[...]).astype(BF16)
    qa = _dot_nt(qn_ref[...], wg)
    qa_ref[...] = qa.reshape(bd, t, KV_LORA_RANK)
    qro_ref[...] = qr_ref[...].astype(F32).reshape(bd, t, LANES)


def _absorb(q, w, layer, bd, t):
    ms = bd * t
    return pl.pallas_call(
        functools.partial(_absorb_kernel, bd=bd, t=t),
        grid=(MLA_HEADS,),
        in_specs=[
            pl.BlockSpec((ms, LANES), lambda h: (0, 2 * h)),
            pl.BlockSpec((ms, LANES), lambda h: (0, 2 * h + 1)),
            pl.BlockSpec((None, KV_LORA_RANK, LANES), lambda h: (layer, 0, h)),
            pl.BlockSpec((None, 1, LANES), lambda h: (layer, 0, 0)),
        ],
        out_specs=[pl.BlockSpec((bd, t, KV_LORA_RANK), lambda h: (0, h, 0)),
                   pl.BlockSpec((bd, t, LANES), lambda h: (0, h, 0))],
        out_shape=[jax.ShapeDtypeStruct((bd, MLA_HEADS * t, KV_LORA_RANK), F32),
                   jax.ShapeDtypeStruct((bd, MLA_HEADS * t, LANES), F32)],
        compiler_params=_params("parallel"),
        name="mla_absorb",
    )(q, q, w["w_uk_f32"], w["k_nope_norm"])


def _mla_sample_kernel(pt_ref, qa_ref, qr_ref, wukt_ref, latn_ref, krn_ref, clat_hbm, ckr_hbm, o_ref,
                       lat_buf, kr_buf, sem, a_sc, latb_sc, new_lat, new_kr, s_sc, *, layer, n_pages, t, tiles):
    b = pl.program_id(0)
    slot = b % 2
    nq = MLA_HEADS * t
    nk = MLA_HEADS * QK_NOPE_DIM

    def page_copies(bb, sl, j):
        pg = pt_ref[bb * n_pages + j]
        row0 = pl.multiple_of(j * PAGE_SIZE, PAGE_SIZE)
        return (pltpu.make_async_copy(clat_hbm.at[layer, pg], lat_buf.at[sl, pl.ds(row0, PAGE_SIZE)], sem.at[0, sl]),
                pltpu.make_async_copy(ckr_hbm.at[layer, pg], kr_buf.at[sl, j], sem.at[1, sl]))

    @pl.when(b == 0)
    def _():
        a_sc[0:nk, :] = wukt_ref[...]

        def body(j, c):
            for cp in page_copies(0, 0, j):
                cp.start()
            return c
        lax.fori_loop(0, n_pages, body, 0)

    nxt = (b + 1) % pl.num_programs(0)

    def wait_body(j, c):
        for cp in page_copies(b, slot, j):
            cp.wait()
        return c
    lax.fori_loop(0, n_pages, wait_body, 0, unroll=_pick(n_pages, (8, 4, 2, 1)))

    a_sc[nk:nk + nq, :] = qa_ref[0].astype(BF16)
    qr = qr_ref[0][:, 0:QK_ROPE_DIM].astype(BF16)

    def nope_scores(latb):
        st = _dot_nt(a_sc[...], latb)
        out = []
        for hd in range(MLA_HEADS):
            kn = st[hd * QK_NOPE_DIM:(hd + 1) * QK_NOPE_DIM, :]
            rn = lax.rsqrt(jnp.mean(kn * kn, axis=0, keepdims=True) + EPS)
            out.append(st[nk + hd * t:nk + (hd + 1) * t, :] * rn)
        return jnp.concatenate(out, axis=0)

    def lane_tile_max(s):
        out = s[:, 0:LANES]
        for c in range(1, s.shape[1] // LANES):
            out = jnp.maximum(out, s[:, c * LANES:(c + 1) * LANES])
        return out

    def lane_tile_sum(p):
        out = p[:, 0:LANES]
        for c in range(1, p.shape[1] // LANES):
            out = out + p[:, c * LANES:(c + 1) * LANES]
        return out

    def score_tile(row0, rows):
        pages = range(row0 // PAGE_SIZE, (row0 + rows) // PAGE_SIZE)
        for j in pages:
            for cp in page_copies(nxt, 1 - slot, j):
                cp.start()
        latb = lat_buf[slot, row0:row0 + rows, :].astype(BF16)
        latb_sc[row0:row0 + rows, :] = latb
        krt = jnp.concatenate([kr_buf[slot, j].astype(BF16) for j in pages], axis=1)
        s = (nope_scores(latb) + _dot(qr, krt)) * (MLA_SCALE * LOG2E)
        s_sc[:, row0:row0 + rows] = s
        return jnp.max(lane_tile_max(s), axis=-1, keepdims=True)

    def merge(state, smax, s, latb):
        m, acc, psum = state
        m_new = jnp.maximum(m, smax)
        corr = jnp.exp2(m - m_new)
        p = jnp.exp2(s - m_new)
        return m_new, acc * corr + _dot(p.astype(BF16), latb), psum * corr + lane_tile_sum(p)

    def merge_tile(state, smax, row0, rows):
        return merge(state, smax, s_sc[:, row0:row0 + rows], latb_sc[row0:row0 + rows, :])

    new_lat[...] = jnp.zeros_like(new_lat)
    new_kr[...] = jnp.zeros_like(new_kr)
    new_lat[0:t, :] = latn_ref[...]
    new_kr[0:t, :] = krn_ref[:, 0:QK_ROPE_DIM]
    newl = new_lat[...].astype(BF16)
    s_new = (nope_scores(newl) + _dot_nt(qr, new_kr[...].astype(BF16))) * (MLA_SCALE * LOG2E)
    rows = lax.broadcasted_iota(jnp.int32, (nq, PAGE_SIZE), 0)
    cols = lax.broadcasted_iota(jnp.int32, (nq, PAGE_SIZE), 1)
    s_new = jnp.where(cols <= rows % t, s_new, -jnp.inf)
    state = (jnp.full((nq, 1), -1e30, F32), jnp.zeros((nq, KV_LORA_RANK), F32), jnp.zeros((nq, LANES), F32))
    state = merge(state, jnp.max(s_new, axis=-1, keepdims=True), s_new, newl)

    smax_prev = score_tile(*tiles[0])
    for i in range(1, len(tiles)):
        smax = score_tile(*tiles[i])
        state = merge_tile(state, smax_prev, *tiles[i - 1])
        smax_prev = smax
    _, acc, psum = merge_tile(state, smax_prev, *tiles[-1])
    o_ref[0] = acc / jnp.sum(psum, axis=-1, keepdims=True)

    @pl.when(b == pl.num_programs(0) - 1)
    def _():
        def body(j, c):
            for cp in page_copies(nxt, 1 - slot, j):
                cp.wait()
            return c
        lax.fori_loop(0, n_pages, body, 0)


def _sample_tiles(tk):
    ts = _pick(tk, (2048, 1024, 512, 256, 128))
    return tuple((r0, ts) for r0 in range(0, tk, ts))


def _mla_sample(pt_flat, qa, qr, lat_new, kr_new, cache_lat, cache_kr_t, w, layer, bd, t, n_pages):
    nq = MLA_HEADS * t
    tk = n_pages * PAGE_SIZE
    grid_spec = pltpu.PrefetchScalarGridSpec(
        num_scalar_prefetch=1,
        grid=(bd,),
        in_specs=[
            pl.BlockSpec((1, nq, KV_LORA_RANK), lambda b, pt: (b, 0, 0)),
            pl.BlockSpec((1, nq, LANES), lambda b, pt: (b, 0, 0)),
            pl.BlockSpec((None, MLA_HEADS * QK_NOPE_DIM, KV_LORA_RANK), lambda b, pt: (layer, 0, 0)),
            pl.BlockSpec((t, KV_LORA_RANK), lambda b, pt: (b, 0)),
            pl.BlockSpec((t, LANES), lambda b, pt: (b, 0)),
            pl.BlockSpec(memory_space=pl.ANY),
            pl.BlockSpec(memory_space=pl.ANY),
        ],
        out_specs=pl.BlockSpec((1, nq, KV_LORA_RANK), lambda b, pt: (b, 0, 0)),
        scratch_shapes=[
            pltpu.VMEM((2, tk, KV_LORA_RANK), F32),
            pltpu.VMEM((2, n_pages, QK_ROPE_DIM, PAGE_SIZE), F32),
            pltpu.SemaphoreType.DMA((2, 2)),
            pltpu.VMEM((MLA_HEADS * QK_NOPE_DIM + nq, KV_LORA_RANK), BF16),
            pltpu.VMEM((tk, KV_LORA_RANK), BF16),
            pltpu.VMEM((PAGE_SIZE, KV_LORA_RANK), F32),
            pltpu.VMEM((PAGE_SIZE, QK_ROPE_DIM), F32),
            pltpu.VMEM((nq, tk), F32),
        ],
    )
    return pl.pallas_call(
        functools.partial(_mla_sample_kernel, layer=layer, n_pages=n_pages, t=t, tiles=_sample_tiles(tk)),
        grid_spec=grid_spec,
        out_shape=jax.ShapeDtypeStruct((bd, nq, KV_LORA_RANK), F32),
        compiler_params=_params("arbitrary"),
        name="mla_sample",
    )(pt_flat, qa, qr, w["w_uk_t"], lat_new, kr_new, cache_lat, cache_kr_t)


def _uv_up_kernel(o_ref, wuv_ref, out_ref, *, bd, t):
    o = o_ref[...].reshape(bd * t, KV_LORA_RANK).astype(BF16)
    out_ref[...] = _dot(o, wuv_ref[...])


def _uv_up(o_lat, w, layer, bd, t):
    return pl.pallas_call(
        functools.partial(_uv_up_kernel, bd=bd, t=t),
        grid=(MLA_HEADS,),
        in_specs=[pl.BlockSpec((bd, t, KV_LORA_RANK), lambda h: (0, h, 0)),
                  pl.BlockSpec((None, KV_LORA_RANK, V_HEAD_DIM), lambda h: (layer, 0, h))],
        out_specs=pl.BlockSpec((bd * t, V_HEAD_DIM), lambda h: (0, h)),
        out_shape=jax.ShapeDtypeStruct((bd * t, MLA_WIDTH), F32),
        compiler_params=_params("parallel"),
        name="mla_uv_up",
    )(o_lat, w["w_uv"])


def _chunk_kernel(u_ref, v_ref, w_ref, b_ref, o_ref):
    rows = lax.broadcasted_iota(jnp.int32, (CHUNK, CHUNK), 0)
    cols = lax.broadcasted_iota(jnp.int32, (CHUNK, CHUNK), 1)
    causal = cols <= rows
    for hd in range(CHUNK_HEADS):
        sl = slice(hd * LANES, (hd + 1) * LANES)
        wm = jnp.where(causal, w_ref[hd], 0.0).astype(BF16)
        bias = b_ref[:, hd:hd + 1]
        for r0 in range(0, u_ref.shape[0], CHUNK):
            rs = slice(r0, r0 + CHUNK)
            s = _dot(wm, v_ref[rs, sl].astype(BF16)) + bias
            o_ref[rs, sl] = u_ref[rs, sl] * s


def _chunk_mlp(u, v, w_sp, b_sp, layer):
    m = u.shape[0]
    tm = _pick(m, (512, 256, 128))
    return pl.pallas_call(
        _chunk_kernel,
        grid=(m // tm,),
        in_specs=[pl.BlockSpec((tm, CHUNK_WIDTH), lambda c: (c, 0)),
                  pl.BlockSpec((tm, CHUNK_WIDTH), lambda c: (c, 0)),
                  pl.BlockSpec((None, CHUNK_HEADS, CHUNK, CHUNK), lambda c: (layer, 0, 0, 0)),
                  pl.BlockSpec((None, CHUNK, CHUNK_HEADS), lambda c: (layer, 0, 0))],
        out_specs=pl.BlockSpec((tm, CHUNK_WIDTH), lambda c: (c, 0)),
        out_shape=jax.ShapeDtypeStruct((m, CHUNK_WIDTH), F32),
        compiler_params=_params("parallel"),
        name="chunk_mlp",
    )(u, v, w_sp, b_sp)


def _mem_kv_kernel(x_ref, g_ref, w_ref, gk_ref, k_ref, v_ref):
    h = _rms(x_ref[...], g_ref[...]).astype(BF16)
    kv = _dot(h, w_ref[...])
    for hd in range(MEM_HEADS):
        sl = slice(hd * LANES, (hd + 1) * LANES)
        k_ref[:, sl] = _rms(kv[:, sl], gk_ref[...])
    v_ref[...] = kv[:, MEM_WIDTH:]


def _mem_kv(mem, w, layer):
    m = mem.shape[0]
    tm = _pick(m, (256, 128))
    return pl.pallas_call(
        _mem_kv_kernel,
        grid=(m // tm,),
        in_specs=[pl.BlockSpec((tm, D_MODEL), lambda i: (i, 0)),
                  pl.BlockSpec((None, 1, D_MODEL), lambda i: (layer, 0, 0)),
                  pl.BlockSpec((None, D_MODEL, 2 * MEM_WIDTH), lambda i: (layer, 0, 0)),
                  pl.BlockSpec((None, 1, LANES), lambda i: (layer, 0, 0))],
        out_specs=[pl.BlockSpec((tm, MEM_WIDTH), lambda i: (i, 0))] * 2,
        out_shape=[jax.ShapeDtypeStruct((m, MEM_WIDTH), F32)] * 2,
        compiler_params=_params("parallel"),
        name="mem_kv",
    )(mem, w["mem_norm"], w["w_mem_kv"], w["mem_k_norm"])


def _softmax_rows(s):
    e = jnp.exp(s - jnp.max(s, axis=-1, keepdims=True))
    return e / jnp.sum(e, axis=-1, keepdims=True)


def _mem_attn_prompt_kernel(q_ref, k_ref, v_ref, o_ref):
    for hd in range(MEM_HEADS):
        sl = slice(hd * LANES, (hd + 1) * LANES)
        s = _dot_nt(q_ref[:, sl].astype(BF16), k_ref[:, sl].astype(BF16)) * MEM_SCALE
        o_ref[:, sl] = _dot(_softmax_rows(s).astype(BF16), v_ref[:, sl].astype(BF16))


def _mem_attn_prompt(qm, mk, mv, batch, seq):
    tq = _pick(seq, (1024, 512, 256, 128))
    nq = seq // tq
    return pl.pallas_call(
        _mem_attn_prompt_kernel,
        grid=(batch, nq),
        in_specs=[pl.BlockSpec((tq, MEM_WIDTH), lambda b, i: (b * nq + i, 0)),
                  pl.BlockSpec((N_MEM, MEM_WIDTH), lambda b, i: (b, 0)),
                  pl.BlockSpec((N_MEM, MEM_WIDTH), lambda b, i: (b, 0))],
        out_specs=pl.BlockSpec((tq, MEM_WIDTH), lambda b, i: (b * nq + i, 0)),
        out_shape=jax.ShapeDtypeStruct((batch * seq, MEM_WIDTH), F32),
        compiler_params=_params("parallel", "parallel"),
        name="mem_attn_prompt",
    )(qm, mk, mv)


def _mem_attn_sample_kernel(q_ref, k_ref, v_ref, o_ref, *, bb, t):
    for hd in range(MEM_HEADS):
        sl = slice(hd * LANES, (hd + 1) * LANES)
        q3 = q_ref[:, sl].reshape(bb, t, MEM_HEAD_DIM).astype(BF16)
        k3 = k_ref[:, pl.ds(hd, N_MEM, stride=MEM_HEADS), :].astype(BF16)
        v3 = v_ref[:, pl.ds(hd, N_MEM, stride=MEM_HEADS), :].astype(BF16)
        s = jnp.einsum("bqd,bkd->bqk", q3, k3, preferred_element_type=F32) * MEM_SCALE
        o3 = jnp.einsum("bqk,bkd->bqd", _softmax_rows(s).astype(BF16), v3, preferred_element_type=F32)
        o_ref[:, sl] = o3.reshape(bb * t, MEM_HEAD_DIM)


def _mem_attn_sample(qm, cache_k, cache_v, layer, bd, t):
    bb = _pick(bd, (16, 8, 4, 2, 1))
    kv_spec = pl.BlockSpec((None, bb, N_MEM * MEM_HEADS, MEM_HEAD_DIM), lambda i: (layer, i, 0, 0))
    return pl.pallas_call(
        functools.partial(_mem_attn_sample_kernel, bb=bb, t=t),
        grid=(bd // bb,),
        in_specs=[pl.BlockSpec((bb * t, MEM_WIDTH), lambda i: (i, 0)), kv_spec, kv_spec],
        out_specs=pl.BlockSpec((bb * t, MEM_WIDTH), lambda i: (i, 0)),
        out_shape=jax.ShapeDtypeStruct((bd * t, MEM_WIDTH), F32),
        compiler_params=_params("parallel"),
        name="mem_attn_sample",
    )(qm, cache_k, cache_v)


def _merge_kernel(x_ref, a_ref, c_ref, m_ref, g_ref, w_ref, o_ref):
    g = g_ref[...]
    c0, c1 = MLA_WIDTH, MLA_WIDTH + CHUNK_WIDTH
    acc = _dot(_rms(a_ref[...], g[:, 0:c0]).astype(BF16), w_ref[0:c0, :])
    acc += _dot(_rms(c_ref[...], g[:, c0:c1]).astype(BF16), w_ref[c0:c1, :])
    acc += _dot(_rms(m_ref[...], g[:, c1:]).astype(BF16), w_ref[c1:, :])
    o_ref[...] = x_ref[...] + acc


def _merge(x, o_mla, o_chunk, o_mem, w, layer):
    m = x.shape[0]
    tm = _pick(m, (512, 256, 128))
    row = lambda n: pl.BlockSpec((tm, n), lambda i: (i, 0))
    return pl.pallas_call(
        _merge_kernel,
        grid=(m // tm,),
        in_specs=[row(D_MODEL), row(MLA_WIDTH), row(CHUNK_WIDTH), row(MEM_WIDTH),
                  pl.BlockSpec((None, 1, D_MODEL), lambda i: (layer, 0, 0)),
                  pl.BlockSpec((None, D_MODEL, D_MODEL), lambda i: (layer, 0, 0))],
        out_specs=row(D_MODEL),
        out_shape=jax.ShapeDtypeStruct((m, D_MODEL), F32),
        compiler_params=_params("parallel"),
        name="merge_heads",
    )(x, o_mla, o_chunk, o_mem, w["out_norm"], w["w_o"])


def _rope_tables(pos):
    inv = ROPE_THETA ** (-jnp.arange(0, QK_ROPE_DIM, 2, dtype=F32) / QK_ROPE_DIM)
    ang = pos.astype(F32)[:, None] * inv[None, :]
    c, s = jnp.cos(ang), jnp.sin(ang)
    z32, z64 = jnp.zeros_like(c), jnp.zeros((pos.shape[0], LANES - QK_ROPE_DIM), F32)
    return (jnp.concatenate([c, c, z64], axis=-1),
            jnp.concatenate([-s, z32, z64], axis=-1),
            jnp.concatenate([z32, s, z64], axis=-1))


def _pad_lanes(a, width):
    return jnp.pad(a, [(0, 0)] * (a.ndim - 1) + [(0, width - a.shape[-1])])


def _prepare_weights(w_in, w_uq, w_uk, w_uv, w_spatial, b_spatial, bd, t, gains):
    depth = w_in.shape[0]
    b = np.cumsum([Q_LORA_RANK, KV_LORA_RANK, QK_ROPE_DIM, CHUNK_WIDTH, CHUNK_WIDTH]).tolist()
    c_q, c_kv, k_r, u, v, q_m = (w_in[..., :b[0]], w_in[..., b[0]:b[1]], w_in[..., b[1]:b[2]],
                                 w_in[..., b[2]:b[3]], w_in[..., b[3]:b[4]], w_in[..., b[4]:])
    w = {}
    w["w_in"] = jnp.concatenate([c_q, c_kv, u, v, q_m, _pad_lanes(k_r, LANES)], axis=-1).astype(BF16)
    uq = w_uq.reshape(depth, Q_LORA_RANK, MLA_HEADS, QK_NOPE_DIM + QK_ROPE_DIM)
    w["w_uq"] = jnp.concatenate(
        [uq[..., :QK_NOPE_DIM].reshape(depth, Q_LORA_RANK, MLA_WIDTH),
         _pad_lanes(uq[..., QK_NOPE_DIM:], LANES).reshape(depth, Q_LORA_RANK, MLA_WIDTH)], axis=-1).astype(BF16)
    uk = w_uk.reshape(depth, KV_LORA_RANK, MLA_HEADS * QK_NOPE_DIM)
    w["w_uk_f32"] = uk
    w["w_uk"] = uk.astype(BF16)
    w["w_uk_t"] = jnp.swapaxes(uk, 1, 2).astype(BF16)
    w["w_uv"] = w_uv.reshape(depth, KV_LORA_RANK, MLA_WIDTH).astype(BF16)
    reps = CHUNK // t
    eye = jnp.eye(reps, dtype=F32)
    w_blk = jnp.einsum("ab,lhts->lhatbs", eye, w_spatial[:, :, :t, :t]).reshape(depth, CHUNK_HEADS, CHUNK, CHUNK)
    w["w_sp_prompt"] = w_spatial
    w["w_sp_sample"] = w_blk
    w["b_sp_prompt"] = jnp.swapaxes(b_spatial, 1, 2)
    w["b_sp_sample"] = jnp.tile(jnp.swapaxes(b_spatial[:, :, :t], 1, 2), (1, reps, 1))
    for name, gval in gains.items():
        gval = gval[:, None, :]
        if gval.shape[-1] < LANES:
            gval = _pad_lanes(gval, LANES)
        w[name] = gval
    return w


def kernel(x_prompt, x_sample, mem_prompt, cache_kv_latent, cache_k_rope, cache_mem_k, cache_mem_v, page_table, ffn1_norm, ffn1_w_gate, ffn1_w_up, ffn1_w_down, mix_norm, w_in, q_lat_norm, w_uq, q_nope_norm, q_rope_norm, kv_lat_norm, w_uk, w_uv, k_nope_norm, k_rope_norm, sgu_norm, w_spatial, b_spatial, mem_norm, w_mem_kv, mem_q_norm, mem_k_norm, out_norm, w_o, ffn2_norm, ffn2_w_gate, ffn2_w_up, ffn2_w_down):
    batch, seq, _ = x_prompt.shape
    bd, t, _ = x_sample.shape
    depth = w_in.shape[0]
    n_pages = page_table.shape[1]
    past_len = n_pages * PAGE_SIZE
    assert (bd * t) % CHUNK == 0 and CHUNK % t == 0 and seq % CHUNK == 0

    gains = dict(mix_norm=mix_norm, q_lat_norm=q_lat_norm, q_nope_norm=q_nope_norm, q_rope_norm=q_rope_norm,
                 kv_lat_norm=kv_lat_norm, k_nope_norm=k_nope_norm, k_rope_norm=k_rope_norm, sgu_norm=sgu_norm,
                 mem_norm=mem_norm, mem_q_norm=mem_q_norm, mem_k_norm=mem_k_norm, out_norm=out_norm,
                 ffn1_norm=ffn1_norm, ffn2_norm=ffn2_norm)
    w = _prepare_weights(w_in, w_uq, w_uk, w_uv, w_spatial, b_spatial, bd, t, gains)
    w["w_mem_kv"] = w_mem_kv.astype(BF16)
    w["w_o"] = w_o.astype(BF16)
    ffn_w = [(w["ffn1_norm"], ffn1_w_gate, ffn1_w_up, ffn1_w_down),
             (w["ffn2_norm"], ffn2_w_gate, ffn2_w_up, ffn2_w_down)]

    tabs_p = _rope_tables(jnp.tile(jnp.arange(seq, dtype=jnp.int32), batch))
    tabs_s = _rope_tables(jnp.tile(past_len + jnp.arange(t, dtype=jnp.int32), bd))
    pt_flat = page_table.reshape(-1)
    cache_mk = cache_mem_k.reshape(depth, bd, N_MEM * MEM_HEADS, MEM_HEAD_DIM)
    cache_mv = cache_mem_v.reshape(depth, bd, N_MEM * MEM_HEADS, MEM_HEAD_DIM)
    cache_kr_t = jnp.swapaxes(cache_k_rope, 2, 3)

    xp = x_prompt.reshape(batch * seq, D_MODEL)
    xs = x_sample.reshape(bd * t, D_MODEL)
    mem = mem_prompt.reshape(batch * N_MEM, D_MODEL)
    outs = [[] for _ in range(7)]
    for l in range(depth):
        xs, w1_bf16 = _ffn(xs, *ffn_w[0], l, True)
        q, lat, kr, u, v, qm = _mixer(xs, tabs_s, w, l, False)
        qa, qr = _absorb(q, w, l, bd, t)
        o_lat = _mla_sample(pt_flat, qa, qr, lat, kr, cache_kv_latent, cache_kr_t, w, l, bd, t, n_pages)
        o_mla = _uv_up(o_lat, w, l, bd, t)
        o_chunk = _chunk_mlp(u, v, w["w_sp_sample"], w["b_sp_sample"], l)
        o_mem = _mem_attn_sample(qm, cache_mk, cache_mv, l, bd, t)
        xs = _merge(xs, o_mla, o_chunk, o_mem, w, l)
        outs[4].append(lat.reshape(bd, t, KV_LORA_RANK))
        outs[5].append(kr[:, :QK_ROPE_DIM].reshape(bd, t, QK_ROPE_DIM))
        outs[6].append(v.reshape(bd, t, CHUNK_WIDTH))
        xs, w2_bf16 = _ffn(xs, *ffn_w[1], l, True)
        xp = _ffn(xp, ffn_w[0][0], *w1_bf16, l, False)
        q, lat, kr, u, v, qm, k, vv = _mixer(xp, tabs_p, w, l, True)
        mk, mv = _mem_kv(mem, w, l)
        o_mla = _mla_prompt(q, k, vv, batch, seq)
        o_chunk = _chunk_mlp(u, v, w["w_sp_prompt"], w["b_sp_prompt"], l)
        o_mem = _mem_attn_prompt(qm, mk, mv, batch, seq)
        xp = _merge(xp, o_mla, o_chunk, o_mem, w, l)
        outs[0].append(lat.reshape(batch, seq, KV_LORA_RANK))
        outs[1].append(kr[:, :QK_ROPE_DIM].reshape(batch, seq, QK_ROPE_DIM))
        outs[2].append(mk.reshape(batch, N_MEM, MEM_HEADS, MEM_HEAD_DIM))
        outs[3].append(mv.reshape(batch, N_MEM, MEM_HEADS, MEM_HEAD_DIM))
        xp = _ffn(xp, ffn_w[1][0], *w2_bf16, l, False)
    return (xp.reshape(batch, seq, D_MODEL), xs.reshape(bd, t, D_MODEL), *[jnp.stack(o) for o in outs])
```
